```python
import math
import jax
import jax.numpy as jnp
from jax import lax
import numpy as np

D_MODEL = 1024
BATCH = 16
SEQ = 2048
DEPTH = 4

MIX_WIDTH = D_MODEL
N_MIXERS = 4
GROUP_WIDTH = MIX_WIDTH // N_MIXERS
HEAD_DIM = 64
CONV_WIDTH = 4
CHUNK = 128

LRU_BLOCKS = GROUP_WIDTH // HEAD_DIM
LRU_C = 8.0

RET_HEADS = GROUP_WIDTH // HEAD_DIM
ROPE_BASE = 10000.0
GN_EPS = 1e-5

RWKV_HEADS = GROUP_WIDTH // HEAD_DIM
RWKV_DECAY_LORA = 64
RWKV_ICLR_LORA = 64
RWKV_GATE_LORA = 128
RWKV_GN_EPS = HEAD_DIM * 1e-5

SSD_HEADS = GROUP_WIDTH // HEAD_DIM
SSD_GROUPS = 2
SSD_STATE = 128
SSD_XBC = GROUP_WIDTH + 2 * SSD_GROUPS * SSD_STATE

MEM_TOKENS = 256
MEM_HEADS = 4
MEM_HEAD_DIM = D_MODEL // MEM_HEADS

FFN_HIDDEN = ((8 * D_MODEL + 3 * 256 - 1) // (3 * 256)) * 256
NORM_EPS = 1e-6

A_COLS = 2 * GROUP_WIDTH
B_COLS = 4 * GROUP_WIDTH
C_COLS = 3 * GROUP_WIDTH + RWKV_DECAY_LORA + RWKV_ICLR_LORA + RWKV_GATE_LORA
D_COLS = GROUP_WIDTH + SSD_XBC + SSD_HEADS
IN_COLS = A_COLS + B_COLS + C_COLS + D_COLS

kernel_name = 'hybrid_parallel_heads_rglru_retnet_rwkv7_ssd'


def rms_norm(x, g):
    xf = x.astype(jnp.float32)
    y = xf * lax.rsqrt(jnp.mean(xf * xf, axis=-1, keepdims=True) + NORM_EPS)
    return (y * g.astype(jnp.float32)).astype(x.dtype)


def head_group_norm(y, w, b, eps):
    bsz, s, h, n = y.shape
    yf = y.astype(jnp.float32)
    mu = jnp.mean(yf, axis=-1, keepdims=True)
    var = jnp.mean(jnp.square(yf - mu), axis=-1, keepdims=True)
    yn = ((yf - mu) * lax.rsqrt(var + eps)).reshape(bsz, s, h * n)
    return yn * w.astype(jnp.float32) + b.astype(jnp.float32)


def causal_conv(x, w, b):
    k, c = w.shape
    y = lax.conv_general_dilated(
        x, w.astype(x.dtype)[:, None, :], window_strides=(1,), padding=[(k - 1, 0)],
        dimension_numbers=('NWC', 'WIO', 'NWC'), feature_group_count=c)
    return y + b.astype(x.dtype)


def linear_scan(a, u):
    def combine(c1, c2):
        a1, b1 = c1
        a2, b2 = c2
        return a1 * a2, a2 * b1 + b2
    _, h = lax.associative_scan(combine, (a, u), axis=1)
    return h


def chunked_decay_attention(q, k, v, log_a):
    b, s, h, n = q.shape
    p = v.shape[-1]
    c = s // CHUNK
    q = q.reshape(b, c, CHUNK, h, n)
    k = k.reshape(b, c, CHUNK, h, n)
    v = v.reshape(b, c, CHUNK, h, p)
    cum = jnp.cumsum(log_a.astype(jnp.float32).reshape(b, c, CHUNK, h), axis=2)
    causal = jnp.tril(jnp.ones((CHUNK, CHUNK), dtype=bool))
    seg = cum[:, :, :, None, :] - cum[:, :, None, :, :]
    decay = jnp.exp(jnp.where(causal[None, None, :, :, None], seg, -jnp.inf))
    scores = jnp.einsum('bcthn,bcshn->bctsh', q, k) * decay
    y_intra = jnp.einsum('bctsh,bcshp->bcthp', scores, v)
    total = cum[:, :, -1]
    w_end = jnp.exp(total[:, :, None, :] - cum)
    chunk_kv = jnp.einsum('bclhn,bclh,bclhp->bchnp', k, w_end, v)

    def step(state, inp):
        kv_c, tot_c = inp
        return state * jnp.exp(tot_c)[..., None, None] + kv_c, state

    init = jnp.zeros((b, h, n, p), jnp.float32)
    _, states_in = lax.scan(step, init, (jnp.moveaxis(chunk_kv, 1, 0), jnp.moveaxis(total, 1, 0)))
    states_in = jnp.moveaxis(states_in, 0, 1)
    y_inter = jnp.einsum('bclhn,bchnp->bclhp', q * jnp.exp(cum)[..., None], states_in)
    return (y_intra + y_inter).reshape(b, s, h, p)


def rotary(x, cos, sin):
    half = x.shape[-1] // 2
    x1, x2 = x[..., :half], x[..., half:]
    c = cos[None, :, None, :]
    s = sin[None, :, None, :]
    return jnp.concatenate([x1 * c - x2 * s, x1 * s + x2 * c], axis=-1)


def rglru_group(p, conv_w, conv_b, w_r, b_r, w_i, b_i, lam):
    b, s, _ = p.shape
    gate, xr = jnp.split(p, 2, axis=-1)
    xr = causal_conv(xr, conv_w, conv_b)
    xb = xr.reshape(b, s, LRU_BLOCKS, HEAD_DIM)
    r = jax.nn.sigmoid(jnp.einsum('bshi,hij->bshj', xb, w_r) + b_r).reshape(b, s, GROUP_WIDTH)
    i = jax.nn.sigmoid(jnp.einsum('bshi,hij->bshj', xb, w_i) + b_i).reshape(b, s, GROUP_WIDTH)
    log_a = (-LRU_C * r.astype(jnp.float32)) * jax.nn.softplus(-lam.astype(jnp.float32))
    a = jnp.exp(log_a)
    u = jnp.sqrt(-jnp.expm1(2.0 * log_a)) * (i * xr).astype(jnp.float32)
    h = linear_scan(a, u)
    return (h * jax.nn.gelu(gate.astype(jnp.float32), approximate=True)).astype(p.dtype)


def retention_group(p, gn_w, gn_b):
    b, s, _ = p.shape
    q, k, v, g = jnp.split(p, 4, axis=-1)
    heads = lambda t: t.reshape(b, s, RET_HEADS, HEAD_DIM)
    pos = jnp.arange(s, dtype=jnp.float32)
    inv_freq = ROPE_BASE ** (-jnp.arange(HEAD_DIM // 2, dtype=jnp.float32) / (HEAD_DIM // 2))
    ang = pos[:, None] * inv_freq[None, :]
    cos, sin = jnp.cos(ang).astype(p.dtype), jnp.sin(ang).astype(p.dtype)
    qh = rotary(heads(q), cos, sin)
    kh = rotary(heads(k), cos, sin) * (HEAD_DIM ** -0.5)
    log_gamma = jnp.log1p(-jnp.exp2(-5.0 - jnp.arange(RET_HEADS, dtype=jnp.float32)))
    y = chunked_decay_attention(qh, kh, heads(v), jnp.broadcast_to(log_gamma, (b, s, RET_HEADS)))
    y = head_group_norm(y, gn_w, gn_b, GN_EPS)
    return (jax.nn.silu(g.astype(jnp.float32)) * y).astype(p.dtype)


def rwkv7_scan(r, w, k, v, kk, a):
    b, s, h, n = r.shape
    seq_first = lambda t: jnp.moveaxis(t.astype(jnp.float32), 1, 0)

    def step(state, inp):
        r_t, w_t, k_t, v_t, kk_t, a_t = inp
        sa = jnp.einsum('bhvk,bhk->bhv', state, -kk_t)
        state = (state * w_t[:, :, None, :] + sa[..., None] * (kk_t * a_t)[:, :, None, :]
                 + v_t[..., None] * k_t[:, :, None, :])
        return state, jnp.einsum('bhvk,bhk->bhv', state, r_t)

    init = jnp.zeros((b, h, n, n), jnp.float32)
    _, y = lax.scan(step, init, tuple(seq_first(t) for t in (r, w, k, v, kk, a)))
    return jnp.moveaxis(y, 0, 1)


def rwkv7_group(p, mu, w0, w2, a0, a2, g2, k_k, k_a, r_k, gn_w, gn_b):
    b, s, _ = p.shape
    shifted = jnp.pad(p, ((0, 0), (1, 0), (0, 0)))[:, :-1]
    p = p + (shifted - p) * mu
    o1 = GROUP_WIDTH
    o3 = 3 * GROUP_WIDTH
    r, k, v, w_lo, a_lo, g_lo = jnp.split(
        p, [o1, 2 * o1, o3, o3 + RWKV_DECAY_LORA, o3 + RWKV_DECAY_LORA + RWKV_ICLR_LORA], axis=-1)
    w = -jax.nn.softplus(-(w0 + jnp.tanh(w_lo) @ w2).astype(jnp.float32)) - 0.5
    decay = jnp.exp(-jnp.exp(w))
    a = jax.nn.sigmoid((a0 + a_lo @ a2).astype(jnp.float32))
    g = jax.nn.sigmoid(g_lo) @ g2
    heads = lambda t: t.reshape(b, s, RWKV_HEADS, HEAD_DIM)
    kk = heads(k * k_k).astype(jnp.float32)
    kk = kk * lax.rsqrt(jnp.sum(kk * kk, axis=-1, keepdims=True) + 1e-12)
    k = k * (1.0 + (a - 1.0) * k_a)
    rh, kh, vh = heads(r), heads(k), heads(v)
    y = rwkv7_scan(rh, heads(decay), kh, vh, kk, heads(a))
    y = head_group_norm(y, gn_w, gn_b, RWKV_GN_EPS)
    bonus = jnp.sum(rh * kh * r_k.reshape(RWKV_HEADS, HEAD_DIM), axis=-1, keepdims=True) * vh
    return ((y + bonus.reshape(b, s, GROUP_WIDTH)) * g).astype(p.dtype)


def ssd_group(p, conv_w, conv_b, dt_bias, a_log, d_skip, norm_w):
    b, s, _ = p.shape
    z, xbc, dt = jnp.split(p, [GROUP_WIDTH, GROUP_WIDTH + SSD_XBC], axis=-1)
    xbc = jax.nn.silu(causal_conv(xbc, conv_w, conv_b))
    xs, bm, cm = jnp.split(xbc, [GROUP_WIDTH, GROUP_WIDTH + SSD_GROUPS * SSD_STATE], axis=-1)
    xs = xs.reshape(b, s, SSD_HEADS, HEAD_DIM)
    rep = SSD_HEADS // SSD_GROUPS
    bm = jnp.repeat(bm.reshape(b, s, SSD_GROUPS, SSD_STATE), rep, axis=2)
    cm = jnp.repeat(cm.reshape(b, s, SSD_GROUPS, SSD_STATE), rep, axis=2)
    dt = jax.nn.softplus(dt.astype(jnp.float32) + dt_bias.astype(jnp.float32))
    a = -jnp.exp(a_log.astype(jnp.float32))
    y = chunked_decay_attention(cm, bm, xs * dt[..., None], dt * a) + d_skip[:, None] * xs
    y = y.reshape(b, s, GROUP_WIDTH) * jax.nn.silu(z.astype(jnp.float32))
    yg = y.reshape(b, s, SSD_GROUPS, GROUP_WIDTH // SSD_GROUPS)
    yg = yg * lax.rsqrt(jnp.mean(yg * yg, axis=-1, keepdims=True) + NORM_EPS)
    return (yg.reshape(b, s, GROUP_WIDTH) * norm_w).astype(p.dtype)


def memory_cross_attention(h, m, wq, wk, wv, wo):
    b, s, _ = h.shape
    n_mem = m.shape[1]
    q = (h @ wq).reshape(b, s, MEM_HEADS, MEM_HEAD_DIM)
    k = (m @ wk).reshape(b, n_mem, MEM_HEADS, MEM_HEAD_DIM)
    v = (m @ wv).reshape(b, n_mem, MEM_HEADS, MEM_HEAD_DIM)
    scores = jnp.einsum('bshd,bmhd->bhsm', q, k).astype(jnp.float32) * (MEM_HEAD_DIM ** -0.5)
    probs = jax.nn.softmax(scores, axis=-1).astype(v.dtype)
    o = jnp.einsum('bhsm,bmhd->bshd', probs, v).reshape(b, s, D_MODEL)
    return o @ wo


def swiglu(h, w_in, w_out):
    gate, up = jnp.split(h @ w_in, 2, axis=-1)
    return (jax.nn.silu(gate) * up) @ w_out


def setup_inputs(seed: int = 0) -> dict:
    key = jax.random.key(seed)
    ks = iter(jax.random.split(key, 64))
    L, G = DEPTH, GROUP_WIDTH

    def normal(shape, scale):
        return jax.random.normal(next(ks), shape, jnp.float32) * scale

    def uniform(shape, lo, hi):
        return jax.random.uniform(next(ks), shape, jnp.float32, lo, hi)

    def gain(shape):
        return 1.0 + normal(shape, 0.02)

    a_target = uniform((L, G), 0.9, 0.999) ** (1.0 / LRU_C)
    lru_lambda = jnp.log(a_target) - jnp.log1p(-a_target)
    dt0 = jnp.exp(uniform((L, SSD_HEADS), math.log(1e-3), math.log(1e-1)))
    ssd_dt_bias = dt0 + jnp.log(-jnp.expm1(-dt0))

    return {
        'x': normal((BATCH, SEQ, D_MODEL), 1.0),
        'mem': normal((BATCH, MEM_TOKENS, D_MODEL), 1.0),
        'norm_mix': gain((L, D_MODEL)),
        'w_in': normal((L, D_MODEL, IN_COLS), D_MODEL ** -0.5),
        'lru_conv_w': normal((L, CONV_WIDTH, G), CONV_WIDTH ** -0.5),
        'lru_conv_b': normal((L, G), 0.02),
        'lru_w_r': normal((L, LRU_BLOCKS, HEAD_DIM, HEAD_DIM), HEAD_DIM ** -0.5),
        'lru_b_r': normal((L, LRU_BLOCKS, HEAD_DIM), 0.02),
        'lru_w_i': normal((L, LRU_BLOCKS, HEAD_DIM, HEAD_DIM), HEAD_DIM ** -0.5),
        'lru_b_i': normal((L, LRU_BLOCKS, HEAD_DIM), 0.02),
        'lru_lambda': lru_lambda,
        'ret_gn_w': gain((L, G)),
        'ret_gn_b': normal((L, G), 0.02),
        'rwkv_mu': uniform((L, C_COLS), 0.0, 1.0),
        'rwkv_w0': uniform((L, G), -6.0, 1.0),
        'rwkv_w2': normal((L, RWKV_DECAY_LORA, G), 0.1),
        'rwkv_a0': normal((L, G), 0.1),
        'rwkv_a2': normal((L, RWKV_ICLR_LORA, G), 0.1),
        'rwkv_g2': normal((L, RWKV_GATE_LORA, G), RWKV_GATE_LORA ** -0.5),
        'rwkv_k_k': 0.85 + normal((L, G), 0.02),
        'rwkv_k_a': gain((L, G)),
        'rwkv_r_k': normal((L, G), 0.1),
        'rwkv_gn_w': gain((L, G)),
        'rwkv_gn_b': normal((L, G), 0.02),
        'ssd_conv_w': normal((L, CONV_WIDTH, SSD_XBC), CONV_WIDTH ** -0.5),
        'ssd_conv_b': normal((L, SSD_XBC), 0.02),
        'ssd_dt_bias': ssd_dt_bias,
        'ssd_a_log': jnp.log(uniform((L, SSD_HEADS), 1.0, 16.0)),
        'ssd_d': gain((L, SSD_HEADS)),
        'ssd_norm_w': gain((L, G)),
        'w_out': normal((L, MIX_WIDTH, D_MODEL), MIX_WIDTH ** -0.5),
        'norm_mem_q': gain((L, D_MODEL)),
        'norm_mem_kv': gain((L, D_MODEL)),
        'mem_wq': normal((L, D_MODEL, D_MODEL), D_MODEL ** -0.5),
        'mem_wk': normal((L, D_MODEL, D_MODEL), D_MODEL ** -0.5),
        'mem_wv': normal((L, D_MODEL, D_MODEL), D_MODEL ** -0.5),
        'mem_wo': normal((L, D_MODEL, D_MODEL), D_MODEL ** -0.5),
        'norm_ffn': gain((L, D_MODEL)),
        'ffn_w_in': normal((L, D_MODEL, 2 * FFN_HIDDEN), D_MODEL ** -0.5),
        'ffn_w_out': normal((L, FFN_HIDDEN, D_MODEL), FFN_HIDDEN ** -0.5),
        'norm_final': gain((D_MODEL,)),
    }


def reference(x, mem, norm_mix, w_in, lru_conv_w, lru_conv_b, lru_w_r, lru_b_r, lru_w_i, lru_b_i,
              lru_lambda, ret_gn_w, ret_gn_b, rwkv_mu, rwkv_w0, rwkv_w2, rwkv_a0, rwkv_a2, rwkv_g2,
              rwkv_k_k, rwkv_k_a, rwkv_r_k, rwkv_gn_w, rwkv_gn_b, ssd_conv_w, ssd_conv_b,
              ssd_dt_bias, ssd_a_log, ssd_d, ssd_norm_w, w_out, norm_mem_q, norm_mem_kv,
              mem_wq, mem_wk, mem_wv, mem_wo, norm_ffn, ffn_w_in, ffn_w_out, norm_final):
    splits = [A_COLS, A_COLS + B_COLS, A_COLS + B_COLS + C_COLS]
    for l in range(DEPTH):
        h = rms_norm(x, norm_mix[l])
        p_a, p_b, p_c, p_d = jnp.split(h @ w_in[l], splits, axis=-1)
        y = jnp.concatenate([
            rglru_group(p_a, lru_conv_w[l], lru_conv_b[l], lru_w_r[l], lru_b_r[l],
                        lru_w_i[l], lru_b_i[l], lru_lambda[l]),
            retention_group(p_b, ret_gn_w[l], ret_gn_b[l]),
            rwkv7_group(p_c, rwkv_mu[l], rwkv_w0[l], rwkv_w2[l], rwkv_a0[l], rwkv_a2[l],
                        rwkv_g2[l], rwkv_k_k[l], rwkv_k_a[l], rwkv_r_k[l],
                        rwkv_gn_w[l], rwkv_gn_b[l]),
            ssd_group(p_d, ssd_conv_w[l], ssd_conv_b[l], ssd_dt_bias[l], ssd_a_log[l],
                      ssd_d[l], ssd_norm_w[l]),
        ], axis=-1)
        x = x + (y @ w_out[l]).astype(x.dtype)
        x = x + memory_cross_attention(rms_norm(x, norm_mem_q[l]), rms_norm(mem, norm_mem_kv[l]),
                                       mem_wq[l], mem_wk[l], mem_wv[l], mem_wo[l]).astype(x.dtype)
        x = x + swiglu(rms_norm(x, norm_ffn[l]), ffn_w_in[l], ffn_w_out[l]).astype(x.dtype)
    return rms_norm(x, norm_final)
```

```python
import functools
import math

import jax
import jax.numpy as jnp
from jax import lax
from jax.experimental import pallas as pl
from jax.experimental.pallas import tpu as pltpu

F32 = jnp.float32
BF16 = jnp.bfloat16

GROUP = 256
HEAD = 64
NHEAD = GROUP // HEAD
CONV_K = 4
LRU_C = 8.0
GN_EPS = 1e-5
RWKV_GN_EPS = HEAD * 1e-5
NORM_EPS = 1e-6
ROPE_BASE = 10000.0
SSD_NGROUPS = 2
SSD_STATE = 128
MEM_HEADS = 4
LOG_GAMMA = tuple(math.log1p(-(2.0 ** (-5.0 - h))) for h in range(NHEAD))

SEQ_CHUNK = 256
RWKV_SUB = 64
ROW_TILE = 512
FFN_COLS = 256
CARRY_ROWS = 8
VMEM_LIMIT = 56 * 1024 * 1024
NEG_BIG = -1e30


def _dot(a, b):
    return jnp.dot(a, b, preferred_element_type=F32)


def _dot_nt(a, b):
    return lax.dot_general(a, b, (((1,), (1,)), ((), ())), preferred_element_type=F32)


def _dot_tn(a, b):
    return lax.dot_general(a, b, (((0,), (0,)), ((), ())), preferred_element_type=F32)


def _split(x, n):
    terms = []
    for _ in range(n):
        t = x.astype(BF16)
        terms.append(t)
        x = x - t.astype(F32)
    return terms


def _dot_split(x, m, n):
    return sum(_dot(t, m) for t in _split(x, n))


def _cumsum_rows(tri, x):
    return sum(_dot(tri, t) for t in _split(x, 3))


def _rms(x, g):
    ms = jnp.mean(x * x, axis=-1, keepdims=True)
    return (x * lax.rsqrt(ms + NORM_EPS)) * g


def _softplus(x):
    return jnp.maximum(x, 0.0) + jnp.log1p(jnp.exp(-jnp.abs(x)))


def _gelu_tanh(x):
    return 0.5 * x * (1.0 + jnp.tanh(math.sqrt(2.0 / math.pi) * (x + 0.044715 * (x * x * x))))


def _lane_head(rows):
    return lax.broadcasted_iota(jnp.int32, (rows, GROUP), 1) // HEAD


def _stack_heads(z, lane_head):
    return jnp.concatenate(
        [jnp.where(lane_head == h, z, 0.0) for h in range(NHEAD)], axis=0).astype(BF16)


def _fold_rows(m, rows):
    out = m[0:rows]
    for h in range(1, NHEAD):
        out = out + m[h * rows:(h + 1) * rows]
    return out


def _causal_conv(buf, cur, cw, cb, rows):
    buf[CARRY_ROWS:CARRY_ROWS + rows, :] = cur
    y = cw[3:4] * cur + cb
    for j in range(CONV_K - 1):
        off = CARRY_ROWS - (CONV_K - 1) + j
        y = y + cw[j:j + 1] * buf[off:off + rows, :]
    buf[0:CARRY_ROWS, :] = buf[rows:rows + CARRY_ROWS, :]
    return y


def _scan_rows(a, u, h0):
    rows, cols = a.shape
    sub = lax.broadcasted_iota(jnp.int32, (8, cols), 0)
    out = []
    h = h0
    for g in range(rows // 8):
        ag = a[g * 8:(g + 1) * 8]
        ug = u[g * 8:(g + 1) * 8]
        for d in (1, 2, 4):
            m = sub >= d
            ug = jnp.where(m, ag * pltpu.roll(ug, d, 0) + ug, ug)
            ag = jnp.where(m, ag * pltpu.roll(ag, d, 0), ag)
        hg = ug + ag * h
        out.append(hg)
        h = hg[7:8]
    return jnp.concatenate(out, axis=0), h


def _group_norm(y, avg, eps, w, b):
    mu = _dot_split(y, avg, 2)
    d = y - mu
    var = _dot_split(d * d, avg, 2)
    return (d * lax.rsqrt(var + eps)) * w + b


def _lru_kernel(x_ref, g_ref, w_ref, cw_ref, cb_ref, wg_ref, bg_ref, lam_ref, o_ref,
                cbuf, hcar):
    rows = x_ref.shape[1]

    @pl.when(pl.program_id(1) == 0)
    def _():
        cbuf[0:CARRY_ROWS, :] = jnp.zeros((CARRY_ROWS, GROUP), F32)
        hcar[...] = jnp.zeros_like(hcar)

    h = _rms(x_ref[0], g_ref[...]).astype(BF16)
    p = _dot(h, w_ref[...])
    gate = p[:, :GROUP]
    xr = _causal_conv(cbuf, p[:, GROUP:], cw_ref[...], cb_ref[...], rows)
    gates = _dot(xr.astype(BF16), wg_ref[...]) + bg_ref[...]
    r = jax.nn.sigmoid(gates[:, :GROUP])
    i = jax.nn.sigmoid(gates[:, GROUP:])
    log_a = (-LRU_C * r) * _softplus(-lam_ref[...])
    a = jnp.exp(log_a)
    u = jnp.sqrt(-jnp.tanh(log_a) * (1.0 + a * a)) * (i * xr)
    hs, h_last = _scan_rows(a, u, hcar[0:1, :])
    hcar[0:1, :] = h_last
    o_ref[0] = (hs * _gelu_tanh(gate)).astype(o_ref.dtype)


def _rotary(z, cos, sin_signed):
    lane = lax.broadcasted_iota(jnp.int32, (z.shape[0], 128), 1)
    first = (lane % HEAD) < (HEAD // 2)
    halves = []
    for s in range(2):
        zh = z[:, s * 128:(s + 1) * 128]
        halves.append(jnp.where(first, pltpu.roll(zh, 128 - HEAD // 2, 1),
                                pltpu.roll(zh, HEAD // 2, 1)))
    return z * cos + jnp.concatenate(halves, axis=1) * sin_signed


def _ret_kernel(x_ref, g_ref, w_ref, cos_ref, sin_ref, gnw_ref, gnb_ref, avg_ref, o_ref,
                state, dmask, qdec, kdec, sdec):
    rows = x_ref.shape[1]
    lane_head = _lane_head(rows)

    @pl.when(pl.program_id(1) == 0)
    def _():
        state[...] = jnp.zeros_like(state)
        t = lax.broadcasted_iota(jnp.int32, (rows, rows), 0)
        s = lax.broadcasted_iota(jnp.int32, (rows, rows), 1)
        dist = (t - s).astype(F32)
        for h in range(NHEAD):
            dmask[h] = jnp.where(t >= s, jnp.exp(dist * LOG_GAMMA[h]), 0.0)
        lg = jnp.zeros((rows, GROUP), F32)
        for h in range(NHEAD):
            lg = jnp.where(lane_head == h, LOG_GAMMA[h], lg)
        tt = lax.broadcasted_iota(jnp.int32, (rows, GROUP), 0).astype(F32)
        qdec[...] = jnp.exp((tt + 1.0) * lg)
        kdec[...] = jnp.exp((rows - 1.0 - tt) * lg)
        row_head = lax.broadcasted_iota(jnp.int32, (GROUP, GROUP), 0) // HEAD
        sd = jnp.zeros((GROUP, GROUP), F32)
        for h in range(NHEAD):
            sd = jnp.where(row_head == h, math.exp(rows * LOG_GAMMA[h]), sd)
        sdec[...] = sd

    h = _rms(x_ref[0], g_ref[...]).astype(BF16)
    p = _dot(h, w_ref[...])
    q = _rotary(p[:, 0:GROUP], cos_ref[...], sin_ref[...])
    k = _rotary(p[:, GROUP:2 * GROUP], cos_ref[...], sin_ref[...]) * (HEAD ** -0.5)
    v = p[:, 2 * GROUP:3 * GROUP]
    gate = p[:, 3 * GROUP:]
    kb = k.astype(BF16)
    probs = []
    for hh in range(NHEAD):
        qm = jnp.where(lane_head == hh, q, 0.0).astype(BF16)
        probs.append((_dot_nt(qm, kb) * dmask[hh]).astype(BF16))
    y = _dot(jnp.concatenate(probs, axis=1), _stack_heads(v, lane_head))
    y = y + _dot((q * qdec[...]).astype(BF16), state[...].astype(BF16))
    kv = _dot_tn((k * kdec[...]).astype(BF16), v.astype(BF16))
    row_head = lax.broadcasted_iota(jnp.int32, (GROUP, GROUP), 0) // HEAD
    col_head = lax.broadcasted_iota(jnp.int32, (GROUP, GROUP), 1) // HEAD
    state[...] = state[...] * sdec[...] + jnp.where(row_head == col_head, kv, 0.0)
    yn = _group_norm(y, avg_ref[...], GN_EPS, gnw_ref[...], gnb_ref[...])
    o_ref[0] = (jax.nn.silu(gate) * yn).astype(o_ref.dtype)


def _rwkv_kernel(x_ref, g_ref, w_ref, mu_ref, w0_ref, a0_ref, wa2_ref, g2_ref, kk_ref, ka_ref,
                 rk_ref, gnw_ref, gnb_ref, avg_ref, ones_ref, tri_ref, o_ref, sbuf, state):
    rows = x_ref.shape[1]
    sub = RWKV_SUB
    blk = NHEAD * sub
    lane_head = _lane_head(sub)

    @pl.when(pl.program_id(1) == 0)
    def _():
        sbuf[0:CARRY_ROWS, :] = jnp.zeros((CARRY_ROWS, sbuf.shape[1]), F32)
        state[...] = jnp.zeros_like(state)

    h = _rms(x_ref[0], g_ref[...]).astype(BF16)
    p = _dot(h, w_ref[...])
    sbuf[CARRY_ROWS:CARRY_ROWS + rows, :] = p
    shifted = sbuf[CARRY_ROWS - 1:CARRY_ROWS - 1 + rows, :]
    sbuf[0:CARRY_ROWS, :] = sbuf[rows:rows + CARRY_ROWS, :]
    p = p + (shifted - p) * mu_ref[...]
    r = p[:, 0:GROUP]
    k = p[:, GROUP:2 * GROUP]
    v = p[:, 2 * GROUP:3 * GROUP]
    wa = p[:, 3 * GROUP:3 * GROUP + 128]
    g_lo = p[:, 3 * GROUP + 128:]
    lora = _dot(jnp.concatenate([jnp.tanh(wa), wa], axis=1).astype(BF16), wa2_ref[...])
    w_log = -_softplus(-(w0_ref[...] + lora[:, :GROUP])) - 0.5
    log_w = -jnp.exp(w_log)
    a = jax.nn.sigmoid(a0_ref[...] + lora[:, GROUP:])
    gate = _dot(jax.nn.sigmoid(g_lo).astype(BF16), g2_ref[...])
    kk = k * kk_ref[...]
    kk = kk * lax.rsqrt(_dot_split(kk * kk, ones_ref[...], 2) + 1e-12)
    k = k * (1.0 + (a - 1.0) * ka_ref[...])
    b = kk * a
    cl = _cumsum_rows(tri_ref[...], log_w)

    r_i = lax.broadcasted_iota(jnp.int32, (blk, blk), 0)
    c_i = lax.broadcasted_iota(jnp.int32, (blk, blk), 1)
    strict = (r_i % sub) > (c_i % sub)
    incl = (r_i % sub) >= (c_i % sub)
    same_head = (r_i // HEAD) == (c_i // HEAD)
    eye = (r_i == c_i).astype(F32)

    s_cur = state[...]
    ys = []
    for j in range(rows // sub):
        sl = slice(j * sub, (j + 1) * sub)
        clj = cl[sl]
        lwj = log_w[sl]
        tot = clj[sub - 1:sub]
        e_neg = jnp.exp(-clj)
        rd = r[sl] * jnp.exp(clj)
        kkd = kk[sl] * jnp.exp(clj - lwj)
        bd = b[sl] * e_neg
        kd = k[sl] * e_neg
        e_end = jnp.exp(tot - clj)
        vj = v[sl]
        lhs = jnp.concatenate([_stack_heads(kkd, lane_head), _stack_heads(rd, lane_head)], axis=0)
        rhs = jnp.concatenate([_stack_heads(bd, lane_head), _stack_heads(kd, lane_head)], axis=0)
        m = _dot_nt(lhs, rhs)
        a_m = jnp.where(strict, m[0:blk, 0:blk], 0.0)
        c_m = jnp.where(strict, m[0:blk, blk:], 0.0)
        ay_m = jnp.where(incl, m[blk:, 0:blk], 0.0)
        cy_m = jnp.where(incl, m[blk:, blk:], 0.0)
        t_m = eye - a_m
        pw = a_m.astype(BF16)
        for _ in range(int(math.log2(sub)) - 1):
            pw = _dot(pw, pw).astype(BF16)
            t_m = t_m + _dot(t_m.astype(BF16), pw)
        t_cat = _fold_rows(t_m, sub).astype(BF16)
        c_cat = _fold_rows(c_m, sub).astype(BF16)
        ay_cat = _fold_rows(ay_m, sub).astype(BF16)
        cy_cat = _fold_rows(cy_m, sub).astype(BF16)
        s_b = s_cur.astype(BF16)
        v_st = _stack_heads(vj, lane_head)
        rhs_u = _dot_nt(kkd.astype(BF16), s_b) + _dot(c_cat, v_st)
        u = -_dot(t_cat, _stack_heads(rhs_u, lane_head))
        y = (_dot_nt(rd.astype(BF16), s_b) + _dot(ay_cat, _stack_heads(u, lane_head))
             + _dot(cy_cat, v_st))
        ys.append(y)
        upd = _dot_tn(jnp.concatenate([u, vj], axis=0).astype(BF16),
                      jnp.concatenate([b[sl] * e_end, k[sl] * e_end], axis=0).astype(BF16))
        s_cur = s_cur * jnp.exp(tot) + jnp.where(same_head, upd, 0.0)
    state[...] = s_cur
    y = jnp.concatenate(ys, axis=0)
    yn = _group_norm(y, avg_ref[...], RWKV_GN_EPS, gnw_ref[...], gnb_ref[...])
    bonus = _dot_split(r * k * rk_ref[...], ones_ref[...], 2) * v
    o_ref[0] = ((yn + bonus) * gate).astype(o_ref.dtype)


def _ssd_kernel(x_ref, g_ref, w_ref, wdt_ref, cw_ref, cb_ref, dtb_ref, alog_ref, dsk_ref, nw_ref,
                tri_ref, esel_ref, o_ref, cbuf, state):
    rows = x_ref.shape[1]
    lane_head = _lane_head(rows)

    @pl.when(pl.program_id(1) == 0)
    def _():
        cbuf[0:CARRY_ROWS, :] = jnp.zeros((CARRY_ROWS, cbuf.shape[1]), F32)
        state[...] = jnp.zeros_like(state)

    h = _rms(x_ref[0], g_ref[...]).astype(BF16)
    p = _dot(h, w_ref[...])
    z = p[:, :GROUP]
    xbc = jax.nn.silu(_causal_conv(cbuf, p[:, GROUP:], cw_ref[...], cb_ref[...], rows))
    xs = xbc[:, :GROUP]
    bm = xbc[:, GROUP:2 * GROUP].astype(BF16)
    cm = xbc[:, 2 * GROUP:].astype(BF16)
    dt = _softplus(_dot(h, wdt_ref[...]) + dtb_ref[...])
    log_a = dt * (-jnp.exp(alog_ref[...]))
    cum = _cumsum_rows(tri_ref[...], log_a)
    tot = cum[rows - 1:rows]
    vdt = xs * dt

    cum_rows = sum(_dot_nt(esel_ref[...], t) for t in _split(cum, 3))
    lane = lax.broadcasted_iota(jnp.int32, (rows, 128), 1)
    low = lane < HEAD
    cols = []
    for s in range(2):
        ch = cum[:, s * 128:(s + 1) * 128]
        sw = pltpu.roll(ch, HEAD, 1)
        cols.append(jnp.where(low, ch, sw))
        cols.append(jnp.where(low, sw, ch))
    t_i = lax.broadcasted_iota(jnp.int32, (rows, rows), 0)
    s_i = lax.broadcasted_iota(jnp.int32, (rows, rows), 1)
    causal = t_i >= s_i
    scores = [_dot_nt(cm[:, g * SSD_STATE:(g + 1) * SSD_STATE], bm[:, g * SSD_STATE:(g + 1) * SSD_STATE])
              for g in range(SSD_NGROUPS)]
    probs = []
    for hh in range(NHEAD):
        col = jnp.concatenate([cols[hh]] * (rows // 128), axis=1)
        row = jnp.broadcast_to(cum_rows[8 * hh:8 * hh + 1, :], (rows, rows))
        decay = jnp.exp(jnp.where(causal, col - row, NEG_BIG))
        probs.append((scores[hh // (NHEAD // SSD_NGROUPS)] * decay).astype(BF16))
    y = _dot(jnp.concatenate(probs, axis=1), _stack_heads(vdt, lane_head))
    y = y + _dot(cm, state[...].astype(BF16)) * jnp.exp(cum)
    upd = _dot_tn(bm, (vdt * jnp.exp(tot - cum)).astype(BF16))
    r_g = lax.broadcasted_iota(jnp.int32, (GROUP, GROUP), 0) // SSD_STATE
    c_g = lax.broadcasted_iota(jnp.int32, (GROUP, GROUP), 1) // SSD_STATE
    state[...] = state[...] * jnp.exp(tot) + jnp.where(r_g == c_g, upd, 0.0)
    y = (y + dsk_ref[...] * xs) * jax.nn.silu(z)
    outs = []
    for g in range(SSD_NGROUPS):
        yg = y[:, g * 128:(g + 1) * 128]
        outs.append(yg * lax.rsqrt(jnp.mean(yg * yg, axis=-1, keepdims=True) + NORM_EPS))
    o_ref[0] = (jnp.concatenate(outs, axis=1) * nw_ref[...]).astype(o_ref.dtype)


def _out_kernel(ya_ref, yb_ref, yc_ref, yd_ref, x_ref, w_ref, o_ref):
    y = jnp.concatenate([ya_ref[...], yb_ref[...], yc_ref[...], yd_ref[...]], axis=1)
    o_ref[...] = x_ref[...] + _dot(y, w_ref[...])


def _kv_kernel(m_ref, g_ref, wk_ref, wv_ref, k_ref, v_ref):
    h = _rms(m_ref[0], g_ref[...]).astype(BF16)
    k_ref[...] = _dot(h, wk_ref[...]).astype(k_ref.dtype)
    v_ref[...] = _dot(h, wv_ref[...]).astype(v_ref.dtype)


def _attn_kernel(x_ref, g_ref, wq_ref, wo_ref, k_ref, v_ref, o_ref):
    x = x_ref[0]
    hd = x.shape[1] // MEM_HEADS
    q = _dot(_rms(x, g_ref[...]).astype(BF16), wq_ref[...])
    outs = []
    for hh in range(MEM_HEADS):
        sl = slice(hh * hd, (hh + 1) * hd)
        s = _dot_nt(q[:, sl].astype(BF16), k_ref[:, sl]) * (hd ** -0.5)
        e = jnp.exp(s - jnp.max(s, axis=-1, keepdims=True))
        pr = e / jnp.sum(e, axis=-1, keepdims=True)
        outs.append(_dot(pr.astype(BF16), v_ref[:, sl]))
    o = jnp.concatenate(outs, axis=1).astype(BF16)
    o_ref[0] = x + _dot(o, wo_ref[...])


def _ffn_kernel(x_ref, g_ref, wi_ref, wo_ref, gf_ref, o_ref, *, hidden, final_norm):
    x = x_ref[...]
    h = _rms(x, g_ref[...]).astype(BF16)
    acc = x
    for c in range(hidden // FFN_COLS):
        lo = c * FFN_COLS
        gate = _dot(h, wi_ref[:, lo:lo + FFN_COLS])
        up = _dot(h, wi_ref[:, hidden + lo:hidden + lo + FFN_COLS])
        acc = acc + _dot((jax.nn.silu(gate) * up).astype(BF16), wo_ref[lo:lo + FFN_COLS, :])
    if final_norm:
        acc = _rms(acc, gf_ref[...])
    o_ref[...] = acc


def _layer_spec(layer, *tail):
    zeros = (0,) * len(tail)
    return pl.BlockSpec((None,) + tail, lambda *_: (layer,) + zeros)


def _const_spec(*shape):
    zeros = (0,) * len(shape)
    return pl.BlockSpec(shape, lambda *_: zeros)


def _mixer_call(body, name, layer, x, params, scratch):
    bsz, seq, dm = x.shape
    chunk = min(SEQ_CHUNK, seq)
    in_specs = [pl.BlockSpec((1, chunk, dm), lambda b, c: (b, c, 0))]
    args = [x]
    for arr, kind in params:
        if kind == "layer":
            in_specs.append(_layer_spec(layer, *arr.shape[1:]))
        elif kind == "seq":
            in_specs.append(pl.BlockSpec((chunk, arr.shape[1]), lambda b, c: (c, 0)))
        else:
            in_specs.append(_const_spec(*arr.shape))
        args.append(arr)
    return pl.pallas_call(
        body,
        grid=(bsz, seq // chunk),
        in_specs=in_specs,
        out_specs=pl.BlockSpec((1, chunk, GROUP), lambda b, c: (b, c, 0)),
        out_shape=jax.ShapeDtypeStruct((bsz, seq, GROUP), BF16),
        scratch_shapes=scratch,
        compiler_params=pltpu.CompilerParams(
            dimension_semantics=("parallel", "arbitrary"), vmem_limit_bytes=VMEM_LIMIT),
        name=name,
    )(*args)


def _head_block_matrix(value):
    idx = jnp.arange(GROUP) // HEAD
    return jnp.where(idx[:, None] == idx[None, :], value, 0.0).astype(BF16)


def kernel(x, mem, norm_mix, w_in, lru_conv_w, lru_conv_b, lru_w_r, lru_b_r, lru_w_i, lru_b_i, lru_lambda, ret_gn_w, ret_gn_b, rwkv_mu, rwkv_w0, rwkv_w2, rwkv_a0, rwkv_a2, rwkv_g2, rwkv_k_k, rwkv_k_a, rwkv_r_k, rwkv_gn_w, rwkv_gn_b, ssd_conv_w, ssd_conv_b, ssd_dt_bias, ssd_a_log, ssd_d, ssd_norm_w, w_out, norm_mem_q, norm_mem_kv, mem_wq, mem_wk, mem_wv, mem_wo, norm_ffn, ffn_w_in, ffn_w_out, norm_final):
    bsz, seq, dm = x.shape
    depth = w_in.shape[0]
    n_mem = mem.shape[1]
    hidden = ffn_w_out.shape[1]
    chunk = min(SEQ_CHUNK, seq)
    tokens = bsz * seq
    row_tile = min(ROW_TILE, tokens)
    assert seq % chunk == 0 and chunk % RWKV_SUB == 0 and tokens % row_tile == 0
    assert seq % min(ROW_TILE, seq) == 0 and hidden % FFN_COLS == 0

    def vec(a):
        return a.reshape(depth, 1, -1).astype(F32)

    def per_head(a):
        return jnp.repeat(a, HEAD, axis=1).reshape(depth, 1, GROUP).astype(F32)

    c_a, c_b, c_c = 2 * GROUP, 6 * GROUP, 10 * GROUP
    c_dt = c_c + 4 * GROUP
    w_a = w_in[:, :, :c_a].astype(BF16)
    w_b = w_in[:, :, c_a:c_b].astype(BF16)
    w_c = w_in[:, :, c_b:c_c].astype(BF16)
    w_d = w_in[:, :, c_c:c_dt].astype(BF16)
    w_dt = jnp.repeat(w_in[:, :, c_dt:], HEAD, axis=2).astype(BF16)
    eye_h = jnp.eye(NHEAD, dtype=F32)

    def block_diag(w):
        return jnp.einsum("lhij,hg->lhigj", w, eye_h).reshape(depth, GROUP, GROUP)

    lru_wg = jnp.concatenate([block_diag(lru_w_r), block_diag(lru_w_i)], axis=2).astype(BF16)
    lru_bg = jnp.concatenate([lru_b_r.reshape(depth, 1, GROUP), lru_b_i.reshape(depth, 1, GROUP)], axis=2)
    n_lo = rwkv_w2.shape[1]
    zeros_lo = jnp.zeros((depth, n_lo, GROUP), F32)
    rwkv_wa2 = jnp.concatenate([
        jnp.concatenate([rwkv_w2, zeros_lo], axis=2),
        jnp.concatenate([zeros_lo, zeros_lo], axis=2),
        jnp.concatenate([zeros_lo, zeros_lo], axis=2),
        jnp.concatenate([zeros_lo, rwkv_a2], axis=2),
    ], axis=1).astype(BF16)

    avg = _head_block_matrix(1.0 / HEAD)
    ones = _head_block_matrix(1.0)
    t_idx = jnp.arange(chunk)
    tri_full = (t_idx[:, None] >= t_idx[None, :]).astype(BF16)
    tri_sub = ((t_idx[:, None] >= t_idx[None, :])
               & (t_idx[:, None] // RWKV_SUB == t_idx[None, :] // RWKV_SUB)).astype(BF16)
    esel = (jnp.arange(GROUP)[None, :] == (jnp.arange(8 * NHEAD)[:, None] // 8) * HEAD).astype(BF16)

    pos = jnp.arange(seq, dtype=F32)
    inv_freq = ROPE_BASE ** (-jnp.arange(HEAD // 2, dtype=F32) / (HEAD // 2))
    ang = pos[:, None] * inv_freq[None, :]
    cos, sin = jnp.cos(ang), jnp.sin(ang)
    cos_t = jnp.tile(jnp.concatenate([cos, cos], axis=1), (1, NHEAD))
    sin_t = jnp.tile(jnp.concatenate([-sin, sin], axis=1), (1, NHEAD))

    w_out_b = w_out.astype(BF16)
    wq_b, wk_b, wv_b, wo_b = (a.astype(BF16) for a in (mem_wq, mem_wk, mem_wv, mem_wo))
    ffn_wi_b = ffn_w_in.astype(BF16)
    ffn_wo_b = ffn_w_out.astype(BF16)
    g_mix, g_q, g_kv, g_ffn = vec(norm_mix), vec(norm_mem_q), vec(norm_mem_kv), vec(norm_ffn)
    g_final = norm_final.reshape(1, dm).astype(F32)
    cparams = pltpu.CompilerParams(dimension_semantics=("parallel",), vmem_limit_bytes=VMEM_LIMIT)

    kv_shape = jax.ShapeDtypeStruct((depth, bsz, n_mem, dm), BF16)
    kv_spec = pl.BlockSpec((None, None, n_mem, dm), lambda l, b: (l, b, 0, 0))
    w_spec = pl.BlockSpec((None, dm, dm), lambda l, b: (l, 0, 0))
    mem_k, mem_v = pl.pallas_call(
        _kv_kernel,
        grid=(depth, bsz),
        in_specs=[pl.BlockSpec((1, n_mem, dm), lambda l, b: (b, 0, 0)),
                  pl.BlockSpec((None, 1, dm), lambda l, b: (l, 0, 0)), w_spec, w_spec],
        out_specs=(kv_spec, kv_spec),
        out_shape=(kv_shape, kv_shape),
        compiler_params=pltpu.CompilerParams(
            dimension_semantics=("parallel", "parallel"), vmem_limit_bytes=VMEM_LIMIT),
        name="mem_kv",
    )(mem, g_kv, wk_b, wv_b)

    conv_scr = lambda width: pltpu.VMEM((chunk + CARRY_ROWS, width), F32)
    sq_state = pltpu.VMEM((GROUP, GROUP), F32)
    q_tile = min(ROW_TILE, seq)

    for l in range(depth):
        lay = lambda a: (a, "layer")
        y_a = _mixer_call(
            _lru_kernel, "mix_lru", l, x,
            [lay(g_mix), lay(w_a), lay(lru_conv_w), lay(vec(lru_conv_b)), lay(lru_wg), lay(lru_bg),
             lay(vec(lru_lambda))],
            [conv_scr(GROUP), pltpu.VMEM((8, GROUP), F32)])
        y_b = _mixer_call(
            _ret_kernel, "mix_ret", l, x,
            [lay(g_mix), lay(w_b), (cos_t, "seq"), (sin_t, "seq"), lay(vec(ret_gn_w)),
             lay(vec(ret_gn_b)), (avg, "const")],
            [sq_state, pltpu.VMEM((NHEAD, chunk, chunk), F32), pltpu.VMEM((chunk, GROUP), F32),
             pltpu.VMEM((chunk, GROUP), F32), sq_state])
        y_c = _mixer_call(
            _rwkv_kernel, "mix_rwkv", l, x,
            [lay(g_mix), lay(w_c), lay(vec(rwkv_mu)), lay(vec(rwkv_w0)), lay(vec(rwkv_a0)),
             lay(rwkv_wa2), lay(rwkv_g2.astype(BF16)), lay(vec(rwkv_k_k)), lay(vec(rwkv_k_a)),
             lay(vec(rwkv_r_k)), lay(vec(rwkv_gn_w)), lay(vec(rwkv_gn_b)), (avg, "const"),
             (ones, "const"), (tri_sub, "const")],
            [conv_scr(4 * GROUP), sq_state])
        y_d = _mixer_call(
            _ssd_kernel, "mix_ssd", l, x,
            [lay(g_mix), lay(w_d), lay(w_dt), lay(ssd_conv_w), lay(vec(ssd_conv_b)),
             lay(per_head(ssd_dt_bias)), lay(per_head(ssd_a_log)), lay(per_head(ssd_d)),
             lay(vec(ssd_norm_w)), (tri_full, "const"), (esel, "const")],
            [conv_scr(3 * GROUP), sq_state])

        x2 = x.reshape(tokens, dm)
        row = lambda width: pl.BlockSpec((row_tile, width), lambda i: (i, 0))
        x2 = pl.pallas_call(
            _out_kernel,
            grid=(tokens // row_tile,),
            in_specs=[row(GROUP)] * 4 + [row(dm), _layer_spec(l, dm, dm)],
            out_specs=row(dm),
            out_shape=jax.ShapeDtypeStruct((tokens, dm), F32),
            compiler_params=cparams,
            name="mix_out",
        )(*(y.reshape(tokens, GROUP) for y in (y_a, y_b, y_c, y_d)), x2, w_out_b)

        kv_l = pl.BlockSpec((None, None, n_mem, dm), lambda b, i, l=l: (l, b, 0, 0))
        x = pl.pallas_call(
            _attn_kernel,
            grid=(bsz, seq // q_tile),
            in_specs=[pl.BlockSpec((1, q_tile, dm), lambda b, i: (b, i, 0)),
                      _layer_spec(l, 1, dm), _layer_spec(l, dm, dm), _layer_spec(l, dm, dm),
                      kv_l, kv_l],
            out_specs=pl.BlockSpec((1, q_tile, dm), lambda b, i: (b, i, 0)),
            out_shape=jax.ShapeDtypeStruct((bsz, seq, dm), F32),
            compiler_params=pltpu.CompilerParams(
                dimension_semantics=("parallel", "parallel"), vmem_limit_bytes=VMEM_LIMIT),
            name="mem_attn",
        )(x2.reshape(bsz, seq, dm), g_q, wq_b, wo_b, mem_k, mem_v)

        x = pl.pallas_call(
            functools.partial(_ffn_kernel, hidden=hidden, final_norm=(l == depth - 1)),
            grid=(tokens // row_tile,),
            in_specs=[row(dm), _layer_spec(l, 1, dm),
                      pl.BlockSpec((None, dm, 2 * hidden), lambda i, l=l: (l, 0, 0),
                                   pipeline_mode=pl.Buffered(1)),
                      pl.BlockSpec((None, hidden, dm), lambda i, l=l: (l, 0, 0),
                                   pipeline_mode=pl.Buffered(1)),
                      _const_spec(1, dm)],
            out_specs=row(dm),
            out_shape=jax.ShapeDtypeStruct((tokens, dm), F32),
            compiler_params=cparams,
            name="ffn",
        )(x.reshape(tokens, dm), g_ffn, ffn_wi_b, ffn_wo_b, g_final).reshape(bsz, seq, dm)
    return x
```

```python
import functools
import math

import jax
import jax.numpy as jnp
from jax import lax
from jax.experimental import pallas as pl
from jax.experimental.pallas import tpu as pltpu

F32 = jnp.float32
BF16 = jnp.bfloat16

GROUP = 256
HEAD = 64
NHEAD = GROUP // HEAD
CONV_K = 4
LRU_C = 8.0
GN_EPS = 1e-5
RWKV_GN_EPS = HEAD * 1e-5
NORM_EPS = 1e-6
ROPE_BASE = 10000.0
SSD_NGROUPS = 2
SSD_STATE = 128
MEM_HEADS = 4
LOG_GAMMA = tuple(math.log1p(-(2.0 ** (-5.0 - h))) for h in range(NHEAD))

SEQ_CHUNK = 256
RWKV_SUB = 64
ROW_TILE = 512
FFN_COLS = 256
CARRY_ROWS = 8
VMEM_LIMIT = 56 * 1024 * 1024
NEG_BIG = -1e30


def _dot(a, b):
    return jnp.dot(a, b, preferred_element_type=F32)


def _dot_nt(a, b):
    return lax.dot_general(a, b, (((1,), (1,)), ((), ())), preferred_element_type=F32)


def _dot_tn(a, b):
    return lax.dot_general(a, b, (((0,), (0,)), ((), ())), preferred_element_type=F32)


def _split(x, n):
    terms = []
    for _ in range(n):
        t = x.astype(BF16)
        terms.append(t)
        x = x - t.astype(F32)
    return terms


def _dot_split(x, m, n):
    return sum(_dot(t, m) for t in _split(x, n))


def _cumsum_rows(tri, x):
    return sum(_dot(tri, t) for t in _split(x, 3))


def _rms(x, g):
    ms = jnp.mean(x * x, axis=-1, keepdims=True)
    return (x * lax.rsqrt(ms + NORM_EPS)) * g


def _softplus(x):
    return jnp.maximum(x, 0.0) + jnp.log1p(jnp.exp(-jnp.abs(x)))


def _gelu_tanh(x):
    return 0.5 * x * (1.0 + jnp.tanh(math.sqrt(2.0 / math.pi) * (x + 0.044715 * (x * x * x))))


def _lane_head(rows):
    return lax.broadcasted_iota(jnp.int32, (rows, GROUP), 1) // HEAD


def _stack_heads(z, lane_head):
    return jnp.concatenate(
        [jnp.where(lane_head == h, z, 0.0) for h in range(NHEAD)], axis=0).astype(BF16)


def _causal_conv(buf, cur, cw, cb, rows):
    buf[CARRY_ROWS:CARRY_ROWS + rows, :] = cur
    y = cw[3:4] * cur + cb
    for j in range(CONV_K - 1):
        off = CARRY_ROWS - (CONV_K - 1) + j
        y = y + cw[j:j + 1] * buf[off:off + rows, :]
    buf[0:CARRY_ROWS, :] = buf[rows:rows + CARRY_ROWS, :]
    return y


def _scan_rows(a, u, h0):
    rows, cols = a.shape
    sub = lax.broadcasted_iota(jnp.int32, (8, cols), 0)
    out = []
    h = h0
    for g in range(rows // 8):
        ag = a[g * 8:(g + 1) * 8]
        ug = u[g * 8:(g + 1) * 8]
        for d in (1, 2, 4):
            m = sub >= d
            ug = jnp.where(m, ag * pltpu.roll(ug, d, 0) + ug, ug)
            ag = jnp.where(m, ag * pltpu.roll(ag, d, 0), ag)
        hg = ug + ag * h
        out.append(hg)
        h = hg[7:8]
    return jnp.concatenate(out, axis=0), h


def _group_norm(y, avg, eps, w, b):
    mu = _dot_split(y, avg, 2)
    d = y - mu
    var = _dot_split(d * d, avg, 2)
    return (d * lax.rsqrt(var + eps)) * w + b


def _lru_kernel(x_ref, g_ref, w_ref, cw_ref, cb_ref, wg_ref, bg_ref, lam_ref, o_ref,
                cbuf, hcar):
    rows = x_ref.shape[1]

    @pl.when(pl.program_id(1) == 0)
    def _():
        cbuf[0:CARRY_ROWS, :] = jnp.zeros((CARRY_ROWS, GROUP), F32)
        hcar[...] = jnp.zeros_like(hcar)

    h = _rms(x_ref[0], g_ref[...]).astype(BF16)
    p = _dot(h, w_ref[...])
    gate = p[:, :GROUP]
    xr = _causal_conv(cbuf, p[:, GROUP:], cw_ref[...], cb_ref[...], rows)
    gates = _dot(xr.astype(BF16), wg_ref[...]) + bg_ref[...]
    r = jax.nn.sigmoid(gates[:, :GROUP])
    i = jax.nn.sigmoid(gates[:, GROUP:])
    log_a = (-LRU_C * r) * _softplus(-lam_ref[...])
    a = jnp.exp(log_a)
    u = jnp.sqrt(-jnp.tanh(log_a) * (1.0 + a * a)) * (i * xr)
    hs, h_last = _scan_rows(a, u, hcar[0:1, :])
    hcar[0:1, :] = h_last
    o_ref[0] = (hs * _gelu_tanh(gate)).astype(o_ref.dtype)


def _rotary(z, cos, sin_signed):
    lane = lax.broadcasted_iota(jnp.int32, (z.shape[0], 128), 1)
    first = (lane % HEAD) < (HEAD // 2)
    halves = []
    for s in range(2):
        zh = z[:, s * 128:(s + 1) * 128]
        halves.append(jnp.where(first, pltpu.roll(zh, 128 - HEAD // 2, 1),
                                pltpu.roll(zh, HEAD // 2, 1)))
    return z * cos + jnp.concatenate(halves, axis=1) * sin_signed


def _ret_kernel(x_ref, g_ref, w_ref, cos_ref, sin_ref, gnw_ref, gnb_ref, avg_ref, o_ref,
                state, dmask, qdec, kdec, sdec):
    rows = x_ref.shape[1]
    lane_head = _lane_head(rows)

    @pl.when(pl.program_id(1) == 0)
    def _():
        state[...] = jnp.zeros_like(state)
        t = lax.broadcasted_iota(jnp.int32, (rows, rows), 0)
        s = lax.broadcasted_iota(jnp.int32, (rows, rows), 1)
        dist = (t - s).astype(F32)
        for h in range(NHEAD):
            dmask[h] = jnp.where(t >= s, jnp.exp(dist * LOG_GAMMA[h]), 0.0)
        lg = jnp.zeros((rows, GROUP), F32)
        for h in range(NHEAD):
            lg = jnp.where(lane_head == h, LOG_GAMMA[h], lg)
        tt = lax.broadcasted_iota(jnp.int32, (rows, GROUP), 0).astype(F32)
        qdec[...] = jnp.exp((tt + 1.0) * lg)
        kdec[...] = jnp.exp((rows - 1.0 - tt) * lg)
        row_head = lax.broadcasted_iota(jnp.int32, (GROUP, GROUP), 0) // HEAD
        sd = jnp.zeros((GROUP, GROUP), F32)
        for h in range(NHEAD):
            sd = jnp.where(row_head == h, math.exp(rows * LOG_GAMMA[h]), sd)
        sdec[...] = sd

    h = _rms(x_ref[0], g_ref[...]).astype(BF16)
    p = _dot(h, w_ref[...])
    q = _rotary(p[:, 0:GROUP], cos_ref[...], sin_ref[...])
    k = _rotary(p[:, GROUP:2 * GROUP], cos_ref[...], sin_ref[...]) * (HEAD ** -0.5)
    v = p[:, 2 * GROUP:3 * GROUP]
    gate = p[:, 3 * GROUP:]
    kb = k.astype(BF16)
    probs = []
    for hh in range(NHEAD):
        qm = jnp.where(lane_head == hh, q, 0.0).astype(BF16)
        probs.append((_dot_nt(qm, kb) * dmask[hh]).astype(BF16))
    y = _dot(jnp.concatenate(probs, axis=1), _stack_heads(v, lane_head))
    y = y + _dot((q * qdec[...]).astype(BF16), state[...].astype(BF16))
    kv = _dot_tn((k * kdec[...]).astype(BF16), v.astype(BF16))
    row_head = lax.broadcasted_iota(jnp.int32, (GROUP, GROUP), 0) // HEAD
    col_head = lax.broadcasted_iota(jnp.int32, (GROUP, GROUP), 1) // HEAD
    state[...] = state[...] * sdec[...] + jnp.where(row_head == col_head, kv, 0.0)
    yn = _group_norm(y, avg_ref[...], GN_EPS, gnw_ref[...], gnb_ref[...])
    o_ref[0] = (jax.nn.silu(gate) * yn).astype(o_ref.dtype)


def _rwkv_kernel(x_ref, g_ref, w_ref, mu_ref, w0_ref, a0_ref, wa2_ref, g2_ref, kk_ref, ka_ref,
                 rk_ref, gnw_ref, gnb_ref, avg_ref, ones_ref, tri_ref, o_ref, sbuf, state):
    rows = x_ref.shape[1]
    sub = RWKV_SUB
    blk = NHEAD * sub
    lane_head = _lane_head(sub)

    @pl.when(pl.program_id(1) == 0)
    def _():
        sbuf[0:CARRY_ROWS, :] = jnp.zeros((CARRY_ROWS, sbuf.shape[1]), F32)
        state[...] = jnp.zeros_like(state)

    h = _rms(x_ref[0], g_ref[...]).astype(BF16)
    p = _dot(h, w_ref[...])
    sbuf[CARRY_ROWS:CARRY_ROWS + rows, :] = p
    shifted = sbuf[CARRY_ROWS - 1:CARRY_ROWS - 1 + rows, :]
    sbuf[0:CARRY_ROWS, :] = sbuf[rows:rows + CARRY_ROWS, :]
    p = p + (shifted - p) * mu_ref[...]
    r = p[:, 0:GROUP]
    k = p[:, GROUP:2 * GROUP]
    v = p[:, 2 * GROUP:3 * GROUP]
    wa = p[:, 3 * GROUP:3 * GROUP + 128]
    g_lo = p[:, 3 * GROUP + 128:]
    lora = _dot(jnp.concatenate([jnp.tanh(wa), wa], axis=1).astype(BF16), wa2_ref[...])
    w_log = -_softplus(-(w0_ref[...] + lora[:, :GROUP])) - 0.5
    log_w = -jnp.exp(w_log)
    a = jax.nn.sigmoid(a0_ref[...] + lora[:, GROUP:])
    gate = _dot(jax.nn.sigmoid(g_lo).astype(BF16), g2_ref[...])
    kk = k * kk_ref[...]
    kk = kk * lax.rsqrt(_dot_split(kk * kk, ones_ref[...], 2) + 1e-12)
    k = k * (1.0 + (a - 1.0) * ka_ref[...])
    b = kk * a
    cl = _cumsum_rows(tri_ref[...], log_w)

    t_f = lax.broadcasted_iota(jnp.int32, (sub, blk), 0)
    s_f = lax.broadcasted_iota(jnp.int32, (sub, blk), 1) % sub
    strict = t_f > s_f
    incl = t_f >= s_f
    eye = (t_f == s_f).astype(F32)
    r_i = lax.broadcasted_iota(jnp.int32, (blk, blk), 0)
    c_i = lax.broadcasted_iota(jnp.int32, (blk, blk), 1)
    same_head = (r_i // HEAD) == (c_i // HEAD)
    same_blk = (r_i // sub) == (c_i // sub)

    def unfold(f):
        return jnp.where(same_blk, jnp.concatenate([f] * NHEAD, axis=0), 0.0).astype(BF16)

    subs = range(rows // sub)
    kkd, rd, bdl, kdl, e_tot, vs, v_st = [], [], [], [], [], [], []
    a_f, c_cat, ay_cat, cy_cat = [], [], [], []
    for j in subs:
        sl = slice(j * sub, (j + 1) * sub)
        clj = cl[sl]
        tot = clj[sub - 1:sub]
        e_neg = jnp.exp(-clj)
        e_end = jnp.exp(tot - clj)
        rd.append(r[sl] * jnp.exp(clj))
        kkd.append(kk[sl] * jnp.exp(clj - log_w[sl]))
        bd = b[sl] * e_neg
        kd = k[sl] * e_neg
        bdl.append(b[sl] * e_end)
        kdl.append(k[sl] * e_end)
        e_tot.append(jnp.exp(tot))
        vs.append(v[sl])
        v_st.append(_stack_heads(v[sl], lane_head))
        lhs = jnp.concatenate([kkd[j], rd[j]], axis=0).astype(BF16)
        rhs = jnp.concatenate([_stack_heads(bd, lane_head), _stack_heads(kd, lane_head)], axis=0)
        m = _dot_nt(lhs, rhs)
        a_f.append(jnp.where(strict, m[0:sub, 0:blk], 0.0))
        c_cat.append(jnp.where(strict, m[0:sub, blk:], 0.0).astype(BF16))
        ay_cat.append(jnp.where(incl, m[sub:, 0:blk], 0.0).astype(BF16))
        cy_cat.append(jnp.where(incl, m[sub:, blk:], 0.0).astype(BF16))
    t_f32 = [eye - a for a in a_f]
    pw = a_f
    pw_bd = [unfold(p_) for p_ in pw]
    for _ in range(int(math.log2(sub)) - 1):
        pw = [_dot(pw[j].astype(BF16), pw_bd[j]) for j in subs]
        pw_bd = [unfold(p_) for p_ in pw]
        t_f32 = [t_f32[j] + _dot(t_f32[j].astype(BF16), pw_bd[j]) for j in subs]
    t_cat = [t.astype(BF16) for t in t_f32]
    tk = [_dot(t_cat[j], _stack_heads(kkd[j], lane_head)) for j in subs]
    cv = [_dot(c_cat[j], v_st[j]) for j in subs]
    tcv = [_dot(t_cat[j], _stack_heads(cv[j], lane_head)) for j in subs]
    g_m = [jnp.where(same_head, _dot_tn(tk[j].astype(BF16), bdl[j].astype(BF16)), 0.0).astype(BF16)
           for j in subs]
    h_m = [jnp.where(same_head,
                     _dot_tn(jnp.concatenate([vs[j], -tcv[j]], axis=0).astype(BF16),
                             jnp.concatenate([kdl[j], bdl[j]], axis=0).astype(BF16)), 0.0)
           for j in subs]
    q_t = [(rd[j] - _dot(ay_cat[j], _stack_heads(tk[j], lane_head))).astype(BF16) for j in subs]
    y_0 = [_dot(cy_cat[j], v_st[j]) - _dot(ay_cat[j], _stack_heads(tcv[j], lane_head)) for j in subs]

    s_cur = state[...]
    ys = []
    for j in subs:
        s_b = s_cur.astype(BF16)
        ys.append(_dot_nt(q_t[j], s_b) + y_0[j])
        s_cur = s_cur * e_tot[j] - _dot(s_b, g_m[j]) + h_m[j]
    state[...] = s_cur
    y = jnp.concatenate(ys, axis=0)
    yn = _group_norm(y, avg_ref[...], RWKV_GN_EPS, gnw_ref[...], gnb_ref[...])
    bonus = _dot_split(r * k * rk_ref[...], ones_ref[...], 2) * v
    o_ref[0] = ((yn + bonus) * gate).astype(o_ref.dtype)


def _ssd_kernel(x_ref, g_ref, w_ref, wdt_ref, cw_ref, cb_ref, dtb_ref, alog_ref, dsk_ref, nw_ref,
                tri_ref, esel_ref, o_ref, cbuf, state):
    rows = x_ref.shape[1]
    lane_head = _lane_head(rows)

    @pl.when(pl.program_id(1) == 0)
    def _():
        cbuf[0:CARRY_ROWS, :] = jnp.zeros((CARRY_ROWS, cbuf.shape[1]), F32)
        state[...] = jnp.zeros_like(state)

    h = _rms(x_ref[0], g_ref[...]).astype(BF16)
    p = _dot(h, w_ref[...])
    z = p[:, :GROUP]
    xbc = jax.nn.silu(_causal_conv(cbuf, p[:, GROUP:], cw_ref[...], cb_ref[...], rows))
    xs = xbc[:, :GROUP]
    bm = xbc[:, GROUP:2 * GROUP].astype(BF16)
    cm = xbc[:, 2 * GROUP:].astype(BF16)
    dt = _softplus(_dot(h, wdt_ref[...]) + dtb_ref[...])
    log_a = dt * (-jnp.exp(alog_ref[...]))
    cum = _cumsum_rows(tri_ref[...], log_a)
    tot = cum[rows - 1:rows]
    vdt = xs * dt

    cum_rows = sum(_dot_nt(esel_ref[...], t) for t in _split(cum, 3))
    lane = lax.broadcasted_iota(jnp.int32, (rows, 128), 1)
    low = lane < HEAD
    cols = []
    for s in range(2):
        ch = cum[:, s * 128:(s + 1) * 128]
        sw = pltpu.roll(ch, HEAD, 1)
        cols.append(jnp.where(low, ch, sw))
        cols.append(jnp.where(low, sw, ch))
    t_i = lax.broadcasted_iota(jnp.int32, (rows, rows), 0)
    s_i = lax.broadcasted_iota(jnp.int32, (rows, rows), 1)
    causal = t_i >= s_i
    scores = [_dot_nt(cm[:, g * SSD_STATE:(g + 1) * SSD_STATE], bm[:, g * SSD_STATE:(g + 1) * SSD_STATE])
              for g in range(SSD_NGROUPS)]
    probs = []
    for hh in range(NHEAD):
        col = jnp.concatenate([cols[hh]] * (rows // 128), axis=1)
        row = jnp.broadcast_to(cum_rows[8 * hh:8 * hh + 1, :], (rows, rows))
        decay = jnp.exp(jnp.where(causal, col - row, NEG_BIG))
        probs.append((scores[hh // (NHEAD // SSD_NGROUPS)] * decay).astype(BF16))
    y = _dot(jnp.concatenate(probs, axis=1), _stack_heads(vdt, lane_head))
    y = y + _dot(cm, state[...].astype(BF16)) * jnp.exp(cum)
    upd = _dot_tn(bm, (vdt * jnp.exp(tot - cum)).astype(BF16))
    r_g = lax.broadcasted_iota(jnp.int32, (GROUP, GROUP), 0) // SSD_STATE
    c_g = lax.broadcasted_iota(jnp.int32, (GROUP, GROUP), 1) // SSD_STATE
    state[...] = state[...] * jnp.exp(tot) + jnp.where(r_g == c_g, upd, 0.0)
    y = (y + dsk_ref[...] * xs) * jax.nn.silu(z)
    outs = []
    for g in range(SSD_NGROUPS):
        yg = y[:, g * 128:(g + 1) * 128]
        outs.append(yg * lax.rsqrt(jnp.mean(yg * yg, axis=-1, keepdims=True) + NORM_EPS))
    o_ref[0] = (jnp.concatenate(outs, axis=1) * nw_ref[...]).astype(o_ref.dtype)


def _kv_kernel(m_ref, g_ref, wk_ref, wv_ref, k_ref, v_ref):
    h = _rms(m_ref[0], g_ref[...]).astype(BF16)
    k_ref[...] = _dot(h, wk_ref[...]).astype(k_ref.dtype)
    v_ref[...] = _dot(h, wv_ref[...]).astype(v_ref.dtype)


def _attn_kernel(ya_ref, yb_ref, yc_ref, yd_ref, x_ref, wmix_ref, g_ref, wq_ref, wo_ref,
                 k_ref, v_ref, o_ref):
    y = jnp.concatenate([ya_ref[0], yb_ref[0], yc_ref[0], yd_ref[0]], axis=1)
    x = x_ref[0] + _dot(y, wmix_ref[...])
    hd = x.shape[1] // MEM_HEADS
    q = _dot(_rms(x, g_ref[...]).astype(BF16), wq_ref[...])
    outs = []
    for hh in range(MEM_HEADS):
        sl = slice(hh * hd, (hh + 1) * hd)
        s = _dot_nt(q[:, sl].astype(BF16), k_ref[:, sl]) * (hd ** -0.5)
        e = jnp.exp(s - jnp.max(s, axis=-1, keepdims=True))
        pr = e / jnp.sum(e, axis=-1, keepdims=True)
        outs.append(_dot(pr.astype(BF16), v_ref[:, sl]))
    o = jnp.concatenate(outs, axis=1).astype(BF16)
    o_ref[0] = x + _dot(o, wo_ref[...])


def _ffn_kernel(x_ref, g_ref, wi_ref, wo_ref, gf_ref, o_ref, *, hidden, final_norm):
    x = x_ref[...]
    h = _rms(x, g_ref[...]).astype(BF16)
    acc = x
    for c in range(hidden // FFN_COLS):
        lo = c * FFN_COLS
        gate = _dot(h, wi_ref[:, lo:lo + FFN_COLS])
        up = _dot(h, wi_ref[:, hidden + lo:hidden + lo + FFN_COLS])
        acc = acc + _dot((jax.nn.silu(gate) * up).astype(BF16), wo_ref[lo:lo + FFN_COLS, :])
    if final_norm:
        acc = _rms(acc, gf_ref[...])
    o_ref[...] = acc


def _layer_spec(layer, *tail):
    zeros = (0,) * len(tail)
    return pl.BlockSpec((None,) + tail, lambda *_: (layer,) + zeros)


def _const_spec(*shape):
    zeros = (0,) * len(shape)
    return pl.BlockSpec(shape, lambda *_: zeros)


def _mixer_call(body, name, layer, x, params, scratch):
    bsz, seq, dm = x.shape
    chunk = min(SEQ_CHUNK, seq)
    in_specs = [pl.BlockSpec((1, chunk, dm), lambda b, c: (b, c, 0))]
    args = [x]
    for arr, kind in params:
        if kind == "layer":
            in_specs.append(_layer_spec(layer, *arr.shape[1:]))
        elif kind == "seq":
            in_specs.append(pl.BlockSpec((chunk, arr.shape[1]), lambda b, c: (c, 0)))
        else:
            in_specs.append(_const_spec(*arr.shape))
        args.append(arr)
    return pl.pallas_call(
        body,
        grid=(bsz, seq // chunk),
        in_specs=in_specs,
        out_specs=pl.BlockSpec((1, chunk, GROUP), lambda b, c: (b, c, 0)),
        out_shape=jax.ShapeDtypeStruct((bsz, seq, GROUP), BF16),
        scratch_shapes=scratch,
        compiler_params=pltpu.CompilerParams(
            dimension_semantics=("parallel", "arbitrary"), vmem_limit_bytes=VMEM_LIMIT),
        name=name,
    )(*args)


def _head_block_matrix(value):
    idx = jnp.arange(GROUP) // HEAD
    return jnp.where(idx[:, None] == idx[None, :], value, 0.0).astype(BF16)


def kernel(x, mem, norm_mix, w_in, lru_conv_w, lru_conv_b, lru_w_r, lru_b_r, lru_w_i, lru_b_i, lru_lambda, ret_gn_w, ret_gn_b, rwkv_mu, rwkv_w0, rwkv_w2, rwkv_a0, rwkv_a2, rwkv_g2, rwkv_k_k, rwkv_k_a, rwkv_r_k, rwkv_gn_w, rwkv_gn_b, ssd_conv_w, ssd_conv_b, ssd_dt_bias, ssd_a_log, ssd_d, ssd_norm_w, w_out, norm_mem_q, norm_mem_kv, mem_wq, mem_wk, mem_wv, mem_wo, norm_ffn, ffn_w_in, ffn_w_out, norm_final):
    bsz, seq, dm = x.shape
    depth = w_in.shape[0]
    n_mem = mem.shape[1]
    hidden = ffn_w_out.shape[1]
    chunk = min(SEQ_CHUNK, seq)
    tokens = bsz * seq
    row_tile = min(ROW_TILE, tokens)
    assert seq % chunk == 0 and chunk % RWKV_SUB == 0 and tokens % row_tile == 0
    assert seq % min(ROW_TILE, seq) == 0 and hidden % FFN_COLS == 0

    def vec(a):
        return a.reshape(depth, 1, -1).astype(F32)

    def per_head(a):
        return jnp.repeat(a, HEAD, axis=1).reshape(depth, 1, GROUP).astype(F32)

    c_a, c_b, c_c = 2 * GROUP, 6 * GROUP, 10 * GROUP
    c_dt = c_c + 4 * GROUP
    w_a = w_in[:, :, :c_a].astype(BF16)
    w_b = w_in[:, :, c_a:c_b].astype(BF16)
    w_c = w_in[:, :, c_b:c_c].astype(BF16)
    w_d = w_in[:, :, c_c:c_dt].astype(BF16)
    w_dt = jnp.repeat(w_in[:, :, c_dt:], HEAD, axis=2).astype(BF16)
    eye_h = jnp.eye(NHEAD, dtype=F32)

    def block_diag(w):
        return jnp.einsum("lhij,hg->lhigj", w, eye_h).reshape(depth, GROUP, GROUP)

    lru_wg = jnp.concatenate([block_diag(lru_w_r), block_diag(lru_w_i)], axis=2).astype(BF16)
    lru_bg = jnp.concatenate([lru_b_r.reshape(depth, 1, GROUP), lru_b_i.reshape(depth, 1, GROUP)], axis=2)
    n_lo = rwkv_w2.shape[1]
    zeros_lo = jnp.zeros((depth, n_lo, GROUP), F32)
    rwkv_wa2 = jnp.concatenate([
        jnp.concatenate([rwkv_w2, zeros_lo], axis=2),
        jnp.concatenate([zeros_lo, zeros_lo], axis=2),
        jnp.concatenate([zeros_lo, zeros_lo], axis=2),
        jnp.concatenate([zeros_lo, rwkv_a2], axis=2),
    ], axis=1).astype(BF16)

    avg = _head_block_matrix(1.0 / HEAD)
    ones = _head_block_matrix(1.0)
    t_idx = jnp.arange(chunk)
    tri_full = (t_idx[:, None] >= t_idx[None, :]).astype(BF16)
    tri_sub = ((t_idx[:, None] >= t_idx[None, :])
               & (t_idx[:, None] // RWKV_SUB == t_idx[None, :] // RWKV_SUB)).astype(BF16)
    esel = (jnp.arange(GROUP)[None, :] == (jnp.arange(8 * NHEAD)[:, None] // 8) * HEAD).astype(BF16)

    pos = jnp.arange(seq, dtype=F32)
    inv_freq = ROPE_BASE ** (-jnp.arange(HEAD // 2, dtype=F32) / (HEAD // 2))
    ang = pos[:, None] * inv_freq[None, :]
    cos, sin = jnp.cos(ang), jnp.sin(ang)
    cos_t = jnp.tile(jnp.concatenate([cos, cos], axis=1), (1, NHEAD))
    sin_t = jnp.tile(jnp.concatenate([-sin, sin], axis=1), (1, NHEAD))

    w_out_b = w_out.astype(BF16)
    wq_b, wk_b, wv_b, wo_b = (a.astype(BF16) for a in (mem_wq, mem_wk, mem_wv, mem_wo))
    ffn_wi_b = ffn_w_in.astype(BF16)
    ffn_wo_b = ffn_w_out.astype(BF16)
    g_mix, g_q, g_kv, g_ffn = vec(norm_mix), vec(norm_mem_q), vec(norm_mem_kv), vec(norm_ffn)
    g_final = norm_final.reshape(1, dm).astype(F32)
    cparams = pltpu.CompilerParams(dimension_semantics=("parallel",), vmem_limit_bytes=VMEM_LIMIT)

    kv_shape = jax.ShapeDtypeStruct((depth, bsz, n_mem, dm), BF16)
    kv_spec = pl.BlockSpec((None, None, n_mem, dm), lambda l, b: (l, b, 0, 0))
    w_spec = pl.BlockSpec((None, dm, dm), lambda l, b: (l, 0, 0))
    mem_k, mem_v = pl.pallas_call(
        _kv_kernel,
        grid=(depth, bsz),
        in_specs=[pl.BlockSpec((1, n_mem, dm), lambda l, b: (b, 0, 0)),
                  pl.BlockSpec((None, 1, dm), lambda l, b: (l, 0, 0)), w_spec, w_spec],
        out_specs=(kv_spec, kv_spec),
        out_shape=(kv_shape, kv_shape),
        compiler_params=pltpu.CompilerParams(
            dimension_semantics=("parallel", "parallel"), vmem_limit_bytes=VMEM_LIMIT),
        name="mem_kv",
    )(mem, g_kv, wk_b, wv_b)

    conv_scr = lambda width: pltpu.VMEM((chunk + CARRY_ROWS, width), F32)
    sq_state = pltpu.VMEM((GROUP, GROUP), F32)
    q_tile = min(ROW_TILE, seq)

    for l in range(depth):
        lay = lambda a: (a, "layer")
        y_a = _mixer_call(
            _lru_kernel, "mix_lru", l, x,
            [lay(g_mix), lay(w_a), lay(lru_conv_w), lay(vec(lru_conv_b)), lay(lru_wg), lay(lru_bg),
             lay(vec(lru_lambda))],
            [conv_scr(GROUP), pltpu.VMEM((8, GROUP), F32)])
        y_b = _mixer_call(
            _ret_kernel, "mix_ret", l, x,
            [lay(g_mix), lay(w_b), (cos_t, "seq"), (sin_t, "seq"), lay(vec(ret_gn_w)),
             lay(vec(ret_gn_b)), (avg, "const")],
            [sq_state, pltpu.VMEM((NHEAD, chunk, chunk), F32), pltpu.VMEM((chunk, GROUP), F32),
             pltpu.VMEM((chunk, GROUP), F32), sq_state])
        y_c = _mixer_call(
            _rwkv_kernel, "mix_rwkv", l, x,
            [lay(g_mix), lay(w_c), lay(vec(rwkv_mu)), lay(vec(rwkv_w0)), lay(vec(rwkv_a0)),
             lay(rwkv_wa2), lay(rwkv_g2.astype(BF16)), lay(vec(rwkv_k_k)), lay(vec(rwkv_k_a)),
             lay(vec(rwkv_r_k)), lay(vec(rwkv_gn_w)), lay(vec(rwkv_gn_b)), (avg, "const"),
             (ones, "const"), (tri_sub, "const")],
            [conv_scr(4 * GROUP), sq_state])
        y_d = _mixer_call(
            _ssd_kernel, "mix_ssd", l, x,
            [lay(g_mix), lay(w_d), lay(w_dt), lay(ssd_conv_w), lay(vec(ssd_conv_b)),
             lay(per_head(ssd_dt_bias)), lay(per_head(ssd_a_log)), lay(per_head(ssd_d)),
             lay(vec(ssd_norm_w)), (tri_full, "const"), (esel, "const")],
            [conv_scr(3 * GROUP), sq_state])

        row = lambda width: pl.BlockSpec((row_tile, width), lambda i: (i, 0))
        q_rows = lambda width: pl.BlockSpec((1, q_tile, width), lambda b, i: (b, i, 0))
        kv_l = pl.BlockSpec((None, None, n_mem, dm), lambda b, i, l=l: (l, b, 0, 0))
        x = pl.pallas_call(
            _attn_kernel,
            grid=(bsz, seq // q_tile),
            in_specs=[q_rows(GROUP)] * 4 + [
                q_rows(dm), _layer_spec(l, dm, dm), _layer_spec(l, 1, dm),
                _layer_spec(l, dm, dm), _layer_spec(l, dm, dm), kv_l, kv_l],
            out_specs=q_rows(dm),
            out_shape=jax.ShapeDtypeStruct((bsz, seq, dm), F32),
            compiler_params=pltpu.CompilerParams(
                dimension_semantics=("parallel", "parallel"), vmem_limit_bytes=VMEM_LIMIT),
            name="mem_attn",
        )(y_a, y_b, y_c, y_d, x, w_out_b, g_q, wq_b, wo_b, mem_k, mem_v)

        x = pl.pallas_call(
            functools.partial(_ffn_kernel, hidden=hidden, final_norm=(l == depth - 1)),
            grid=(tokens // row_tile,),
            in_specs=[row(dm), _layer_spec(l, 1, dm),
                      pl.BlockSpec((None, dm, 2 * hidden), lambda i, l=l: (l, 0, 0),
                                   pipeline_mode=pl.Buffered(1)),
                      pl.BlockSpec((None, hidden, dm), lambda i, l=l: (l, 0, 0),
                                   pipeline_mode=pl.Buffered(1)),
                      _const_spec(1, dm)],
            out_specs=row(dm),
            out_shape=jax.ShapeDtypeStruct((tokens, dm), F32),
            compiler_params=cparams,
            name="ffn",
        )(x.reshape(tokens, dm), g_ffn, ffn_wi_b, ffn_wo_b, g_final).reshape(bsz, seq, dm)
    return x
```

```python
import functools
import math

import jax
import jax.numpy as jnp
from jax import lax
from jax.experimental import pallas as pl
from jax.experimental.pallas import tpu as pltpu

F32 = jnp.float32
BF16 = jnp.bfloat16

GROUP = 256
HEAD = 64
NHEAD = GROUP // HEAD
CONV_K = 4
LRU_C = 8.0
GN_EPS = 1e-5
RWKV_GN_EPS = HEAD * 1e-5
NORM_EPS = 1e-6
ROPE_BASE = 10000.0
SSD_NGROUPS = 2
SSD_STATE = 128
MEM_HEADS = 4
LOG_GAMMA = tuple(math.log1p(-(2.0 ** (-5.0 - h))) for h in range(NHEAD))

SEQ_CHUNK = 256
RWKV_SUB = 64
RWKV_SEQS = 2
ROW_TILE = 512
FFN_COLS = 256
CARRY_ROWS = 8
VMEM_LIMIT = 56 * 1024 * 1024
NEG_BIG = -1e30


def _dot(a, b):
    return jnp.dot(a, b, preferred_element_type=F32)


def _dot_nt(a, b):
    return lax.dot_general(a, b, (((1,), (1,)), ((), ())), preferred_element_type=F32)


def _dot_tn(a, b):
    return lax.dot_general(a, b, (((0,), (0,)), ((), ())), preferred_element_type=F32)


def _split(x, n):
    terms = []
    for _ in range(n):
        t = x.astype(BF16)
        terms.append(t)
        x = x - t.astype(F32)
    return terms


def _cumsum_rows(tri, x):
    return sum(_dot(tri, t) for t in _split(x, 3))


def _rms(x, g):
    ms = jnp.mean(x * x, axis=-1, keepdims=True)
    return (x * lax.rsqrt(ms + NORM_EPS)) * g


def _softplus(x):
    return jnp.maximum(x, 0.0) + jnp.log1p(jnp.exp(-jnp.abs(x)))


def _gelu_tanh(x):
    return 0.5 * x * (1.0 + jnp.tanh(math.sqrt(2.0 / math.pi) * (x + 0.044715 * (x * x * x))))


def _lane_head(rows):
    return lax.broadcasted_iota(jnp.int32, (rows, GROUP), 1) // HEAD


def _stack_heads(z, lane_head):
    return jnp.concatenate(
        [jnp.where(lane_head == h, z, 0.0) for h in range(NHEAD)], axis=0).astype(BF16)


def _head_sum(x, block):
    return _dot(x.astype(BF16), block)


def _causal_conv(buf, cur, cw, cb, rows):
    buf[CARRY_ROWS:CARRY_ROWS + rows, :] = cur
    y = cw[3:4] * cur + cb
    for j in range(CONV_K - 1):
        off = CARRY_ROWS - (CONV_K - 1) + j
        y = y + cw[j:j + 1] * buf[off:off + rows, :]
    buf[0:CARRY_ROWS, :] = buf[rows:rows + CARRY_ROWS, :]
    return y


def _scan_rows(a, u, h0):
    rows, cols = a.shape
    sub = lax.broadcasted_iota(jnp.int32, (8, cols), 0)
    out = []
    h = h0
    for g in range(rows // 8):
        ag = a[g * 8:(g + 1) * 8]
        ug = u[g * 8:(g + 1) * 8]
        for d in (1, 2, 4):
            m = sub >= d
            ug = jnp.where(m, ag * pltpu.roll(ug, d, 0) + ug, ug)
            ag = jnp.where(m, ag * pltpu.roll(ag, d, 0), ag)
        hg = ug + ag * h
        out.append(hg)
        h = hg[7:8]
    return jnp.concatenate(out, axis=0), h


def _group_norm(y, avg, eps, w, b):
    mu = _head_sum(y, avg)
    d = y - mu
    var = _head_sum(d * d, avg)
    return (d * lax.rsqrt(var + eps)) * w + b


def _prenorm_kernel(x_ref, g_ref, o_ref):
    o_ref[...] = _rms(x_ref[...], g_ref[...]).astype(o_ref.dtype)


def _lru_kernel(h_ref, w_ref, cw_ref, cb_ref, wg_ref, bg_ref, lam_ref, o_ref, cbuf, hcar):
    rows = h_ref.shape[1]

    @pl.when(pl.program_id(1) == 0)
    def _():
        cbuf[0:CARRY_ROWS, :] = jnp.zeros((CARRY_ROWS, GROUP), F32)
        hcar[...] = jnp.zeros_like(hcar)

    p = _dot(h_ref[0], w_ref[...])
    gate = p[:, :GROUP]
    xr = _causal_conv(cbuf, p[:, GROUP:], cw_ref[...], cb_ref[...], rows)
    gates = _dot(xr.astype(BF16), wg_ref[...]) + bg_ref[...]
    r = jax.nn.sigmoid(gates[:, :GROUP])
    i = jax.nn.sigmoid(gates[:, GROUP:])
    log_a = (-LRU_C * r) * _softplus(-lam_ref[...])
    a = jnp.exp(log_a)
    u = jnp.sqrt(-jnp.tanh(log_a) * (1.0 + a * a)) * (i * xr)
    hs, h_last = _scan_rows(a, u, hcar[0:1, :])
    hcar[0:1, :] = h_last
    o_ref[0] = (hs * _gelu_tanh(gate)).astype(o_ref.dtype)


def _rotary(z, cos, sin_signed):
    lane = lax.broadcasted_iota(jnp.int32, (z.shape[0], 128), 1)
    first = (lane % HEAD) < (HEAD // 2)
    halves = []
    for s in range(2):
        zh = z[:, s * 128:(s + 1) * 128]
        halves.append(jnp.where(first, pltpu.roll(zh, 128 - HEAD // 2, 1),
                                pltpu.roll(zh, HEAD // 2, 1)))
    return z * cos + jnp.concatenate(halves, axis=1) * sin_signed


def _ret_kernel(h_ref, w_ref, cos_ref, sin_ref, gnw_ref, gnb_ref, avg_ref, o_ref,
                state, dmask, qdec, kdec, sdec):
    rows = h_ref.shape[1]
    lane_head = _lane_head(rows)

    @pl.when(pl.program_id(1) == 0)
    def _():
        state[...] = jnp.zeros_like(state)
        t = lax.broadcasted_iota(jnp.int32, (rows, rows), 0)
        s = lax.broadcasted_iota(jnp.int32, (rows, rows), 1)
        dist = (t - s).astype(F32)
        for h in range(NHEAD):
            dmask[h] = jnp.where(t >= s, jnp.exp(dist * LOG_GAMMA[h]), 0.0)
        lg = jnp.zeros((rows, GROUP), F32)
        for h in range(NHEAD):
            lg = jnp.where(lane_head == h, LOG_GAMMA[h], lg)
        tt = lax.broadcasted_iota(jnp.int32, (rows, GROUP), 0).astype(F32)
        qdec[...] = jnp.exp((tt + 1.0) * lg)
        kdec[...] = jnp.exp((rows - 1.0 - tt) * lg)
        row_head = lax.broadcasted_iota(jnp.int32, (GROUP, GROUP), 0) // HEAD
        sd = jnp.zeros((GROUP, GROUP), F32)
        for h in range(NHEAD):
            sd = jnp.where(row_head == h, math.exp(rows * LOG_GAMMA[h]), sd)
        sdec[...] = sd

    p = _dot(h_ref[0], w_ref[...])
    q = _rotary(p[:, 0:GROUP], cos_ref[...], sin_ref[...])
    k = _rotary(p[:, GROUP:2 * GROUP], cos_ref[...], sin_ref[...]) * (HEAD ** -0.5)
    v = p[:, 2 * GROUP:3 * GROUP]
    gate = p[:, 3 * GROUP:]
    kb = k.astype(BF16)
    probs = []
    for hh in range(NHEAD):
        qm = jnp.where(lane_head == hh, q, 0.0).astype(BF16)
        probs.append((_dot_nt(qm, kb) * dmask[hh]).astype(BF16))
    y = _dot(jnp.concatenate(probs, axis=1), _stack_heads(v, lane_head))
    y = y + _dot((q * qdec[...]).astype(BF16), state[...].astype(BF16))
    kv = _dot_tn((k * kdec[...]).astype(BF16), v.astype(BF16))
    row_head = lax.broadcasted_iota(jnp.int32, (GROUP, GROUP), 0) // HEAD
    col_head = lax.broadcasted_iota(jnp.int32, (GROUP, GROUP), 1) // HEAD
    state[...] = state[...] * sdec[...] + jnp.where(row_head == col_head, kv, 0.0)
    yn = _group_norm(y, avg_ref[...], GN_EPS, gnw_ref[...], gnb_ref[...])
    o_ref[0] = (jax.nn.silu(gate) * yn).astype(o_ref.dtype)


def _rwkv_kernel(h_ref, w_ref, mu_ref, w0_ref, a0_ref, wa2_ref, g2_ref, kk_ref, ka_ref,
                 rk_ref, gnw_ref, gnb_ref, avg_ref, ones_ref, tri_ref, o_ref, sbuf, state):
    nseq, rows, dm = h_ref.shape
    sub = RWKV_SUB
    blk = NHEAD * sub
    per_seq = rows // sub
    lane_head = _lane_head(sub)

    @pl.when(pl.program_id(1) == 0)
    def _():
        for b_i in range(nseq):
            sbuf[b_i, 0:CARRY_ROWS, :] = jnp.zeros((CARRY_ROWS, sbuf.shape[2]), F32)
        state[...] = jnp.zeros_like(state)

    p = _dot(h_ref[...].reshape(nseq * rows, dm), w_ref[...])
    shifted = []
    for b_i in range(nseq):
        sbuf[b_i, CARRY_ROWS:CARRY_ROWS + rows, :] = p[b_i * rows:(b_i + 1) * rows]
        shifted.append(sbuf[b_i, CARRY_ROWS - 1:CARRY_ROWS - 1 + rows, :])
        sbuf[b_i, 0:CARRY_ROWS, :] = sbuf[b_i, rows:rows + CARRY_ROWS, :]
    p = p + (jnp.concatenate(shifted, axis=0) - p) * mu_ref[...]
    r = p[:, 0:GROUP]
    k = p[:, GROUP:2 * GROUP]
    v = p[:, 2 * GROUP:3 * GROUP]
    wa = p[:, 3 * GROUP:3 * GROUP + 128]
    g_lo = p[:, 3 * GROUP + 128:]
    lora = _dot(jnp.concatenate([jnp.tanh(wa), wa], axis=1).astype(BF16), wa2_ref[...])
    w_log = -_softplus(-(w0_ref[...] + lora[:, :GROUP])) - 0.5
    log_w = -jnp.exp(w_log)
    a = jax.nn.sigmoid(a0_ref[...] + lora[:, GROUP:])
    gate = _dot(jax.nn.sigmoid(g_lo).astype(BF16), g2_ref[...])
    kk = k * kk_ref[...]
    kk = kk * lax.rsqrt(_head_sum(kk * kk, ones_ref[...]) + 1e-12)
    k = k * (1.0 + (a - 1.0) * ka_ref[...])
    b = kk * a
    cl = jnp.concatenate(
        [_cumsum_rows(tri_ref[...], log_w[b_i * rows:(b_i + 1) * rows]) for b_i in range(nseq)],
        axis=0)

    t_f = lax.broadcasted_iota(jnp.int32, (sub, blk), 0)
    s_f = lax.broadcasted_iota(jnp.int32, (sub, blk), 1) % sub
    strict = t_f > s_f
    incl = t_f >= s_f
    eye = (t_f == s_f).astype(F32)
    r_i = lax.broadcasted_iota(jnp.int32, (blk, blk), 0)
    c_i = lax.broadcasted_iota(jnp.int32, (blk, blk), 1)
    same_head = (r_i // HEAD) == (c_i // HEAD)
    same_blk = (r_i // sub) == (c_i // sub)

    def unfold(f):
        return jnp.where(same_blk, jnp.concatenate([f] * NHEAD, axis=0), 0.0).astype(BF16)

    subs = range(nseq * per_seq)
    kkd, rd, bdl, kdl, e_tot, vs, v_st = [], [], [], [], [], [], []
    a_f, c_cat, ay_cat, cy_cat = [], [], [], []
    for j in subs:
        sl = slice(j * sub, (j + 1) * sub)
        clj = cl[sl]
        tot = clj[sub - 1:sub]
        e_neg = jnp.exp(-clj)
        e_end = jnp.exp(tot - clj)
        rd.append(r[sl] * jnp.exp(clj))
        kkd.append(kk[sl] * jnp.exp(clj - log_w[sl]))
        bd = b[sl] * e_neg
        kd = k[sl] * e_neg
        bdl.append(b[sl] * e_end)
        kdl.append(k[sl] * e_end)
        e_tot.append(jnp.exp(tot))
        vs.append(v[sl])
        v_st.append(_stack_heads(v[sl], lane_head))
        lhs = jnp.concatenate([kkd[j], rd[j]], axis=0).astype(BF16)
        rhs = jnp.concatenate([_stack_heads(bd, lane_head), _stack_heads(kd, lane_head)], axis=0)
        m = _dot_nt(lhs, rhs)
        a_f.append(jnp.where(strict, m[0:sub, 0:blk], 0.0))
        c_cat.append(jnp.where(strict, m[0:sub, blk:], 0.0).astype(BF16))
        ay_cat.append(jnp.where(incl, m[sub:, 0:blk], 0.0).astype(BF16))
        cy_cat.append(jnp.where(incl, m[sub:, blk:], 0.0).astype(BF16))
    t_f32 = [eye - a_ for a_ in a_f]
    pw = a_f
    pw_bd = [unfold(p_) for p_ in pw]
    for _ in range(int(math.log2(sub)) - 1):
        pw = [_dot(pw[j].astype(BF16), pw_bd[j]) for j in subs]
        pw_bd = [unfold(p_) for p_ in pw]
        t_f32 = [t_f32[j] + _dot(t_f32[j].astype(BF16), pw_bd[j]) for j in subs]
    t_cat = [t.astype(BF16) for t in t_f32]
    tk = [_dot(t_cat[j], _stack_heads(kkd[j], lane_head)) for j in subs]
    cv = [_dot(c_cat[j], v_st[j]) for j in subs]
    tcv = [_dot(t_cat[j], _stack_heads(cv[j], lane_head)) for j in subs]
    g_m = [jnp.where(same_head, _dot_tn(tk[j].astype(BF16), bdl[j].astype(BF16)), 0.0).astype(BF16)
           for j in subs]
    h_m = [jnp.where(same_head,
                     _dot_tn(jnp.concatenate([vs[j], -tcv[j]], axis=0).astype(BF16),
                             jnp.concatenate([kdl[j], bdl[j]], axis=0).astype(BF16)), 0.0)
           for j in subs]
    q_t = [(rd[j] - _dot(ay_cat[j], _stack_heads(tk[j], lane_head))).astype(BF16) for j in subs]
    y_0 = [_dot(cy_cat[j], v_st[j]) - _dot(ay_cat[j], _stack_heads(tcv[j], lane_head)) for j in subs]

    s_cur = [state[b_i] for b_i in range(nseq)]
    ys = [None] * len(subs)
    for jj in range(per_seq):
        for b_i in range(nseq):
            j = b_i * per_seq + jj
            s_b = s_cur[b_i].astype(BF16)
            ys[j] = _dot_nt(q_t[j], s_b) + y_0[j]
            s_cur[b_i] = s_cur[b_i] * e_tot[j] - _dot(s_b, g_m[j]) + h_m[j]
    for b_i in range(nseq):
        state[b_i] = s_cur[b_i]
    y = jnp.concatenate(ys, axis=0)
    yn = _group_norm(y, avg_ref[...], RWKV_GN_EPS, gnw_ref[...], gnb_ref[...])
    bonus = _head_sum(r * k * rk_ref[...], ones_ref[...]) * v
    out = ((yn + bonus) * gate).astype(o_ref.dtype)
    for b_i in range(nseq):
        o_ref[b_i] = out[b_i * rows:(b_i + 1) * rows]


def _ssd_kernel(h_ref, w_ref, wdt_ref, cw_ref, cb_ref, dtb_ref, alog_ref, dsk_ref, nw_ref,
                tri_ref, esel_ref, o_ref, cbuf, state):
    rows = h_ref.shape[1]
    lane_head = _lane_head(rows)

    @pl.when(pl.program_id(1) == 0)
    def _():
        cbuf[0:CARRY_ROWS, :] = jnp.zeros((CARRY_ROWS, cbuf.shape[1]), F32)
        state[...] = jnp.zeros_like(state)

    h = h_ref[0]
    p = _dot(h, w_ref[...])
    z = p[:, :GROUP]
    xbc = jax.nn.silu(_causal_conv(cbuf, p[:, GROUP:], cw_ref[...], cb_ref[...], rows))
    xs = xbc[:, :GROUP]
    bm = xbc[:, GROUP:2 * GROUP].astype(BF16)
    cm = xbc[:, 2 * GROUP:].astype(BF16)
    dt = _softplus(_dot(h, wdt_ref[...]) + dtb_ref[...])
    log_a = dt * (-jnp.exp(alog_ref[...]))
    cum = _cumsum_rows(tri_ref[...], log_a)
    tot = cum[rows - 1:rows]
    vdt = xs * dt

    cum_rows = sum(_dot_nt(esel_ref[...], t) for t in _split(cum, 3))
    lane = lax.broadcasted_iota(jnp.int32, (rows, 128), 1)
    low = lane < HEAD
    cols = []
    for s in range(2):
        ch = cum[:, s * 128:(s + 1) * 128]
        sw = pltpu.roll(ch, HEAD, 1)
        cols.append(jnp.where(low, ch, sw))
        cols.append(jnp.where(low, sw, ch))
    t_i = lax.broadcasted_iota(jnp.int32, (rows, rows), 0)
    s_i = lax.broadcasted_iota(jnp.int32, (rows, rows), 1)
    causal = t_i >= s_i
    scores = [_dot_nt(cm[:, g * SSD_STATE:(g + 1) * SSD_STATE], bm[:, g * SSD_STATE:(g + 1) * SSD_STATE])
              for g in range(SSD_NGROUPS)]
    probs = []
    for hh in range(NHEAD):
        col = jnp.concatenate([cols[hh]] * (rows // 128), axis=1)
        row = jnp.broadcast_to(cum_rows[8 * hh:8 * hh + 1, :], (rows, rows))
        decay = jnp.exp(jnp.where(causal, col - row, NEG_BIG))
        probs.append((scores[hh // (NHEAD // SSD_NGROUPS)] * decay).astype(BF16))
    y = _dot(jnp.concatenate(probs, axis=1), _stack_heads(vdt, lane_head))
    y = y + _dot(cm, state[...].astype(BF16)) * jnp.exp(cum)
    upd = _dot_tn(bm, (vdt * jnp.exp(tot - cum)).astype(BF16))
    r_g = lax.broadcasted_iota(jnp.int32, (GROUP, GROUP), 0) // SSD_STATE
    c_g = lax.broadcasted_iota(jnp.int32, (GROUP, GROUP), 1) // SSD_STATE
    state[...] = state[...] * jnp.exp(tot) + jnp.where(r_g == c_g, upd, 0.0)
    y = (y + dsk_ref[...] * xs) * jax.nn.silu(z)
    outs = []
    for g in range(SSD_NGROUPS):
        yg = y[:, g * 128:(g + 1) * 128]
        outs.append(yg * lax.rsqrt(jnp.mean(yg * yg, axis=-1, keepdims=True) + NORM_EPS))
    o_ref[0] = (jnp.concatenate(outs, axis=1) * nw_ref[...]).astype(o_ref.dtype)


def _kv_kernel(m_ref, g_ref, wk_ref, wv_ref, k_ref, v_ref):
    h = _rms(m_ref[0], g_ref[...]).astype(BF16)
    k_ref[...] = _dot(h, wk_ref[...]).astype(k_ref.dtype)
    v_ref[...] = _dot(h, wv_ref[...]).astype(v_ref.dtype)


def _attn_kernel(ya_ref, yb_ref, yc_ref, yd_ref, x_ref, wmix_ref, g_ref, wq_ref, wo_ref,
                 k_ref, v_ref, o_ref):
    y = jnp.concatenate([ya_ref[0], yb_ref[0], yc_ref[0], yd_ref[0]], axis=1)
    x = x_ref[0] + _dot(y, wmix_ref[...])
    hd = x.shape[1] // MEM_HEADS
    q = _dot(_rms(x, g_ref[...]).astype(BF16), wq_ref[...])
    outs = []
    for hh in range(MEM_HEADS):
        sl = slice(hh * hd, (hh + 1) * hd)
        s = _dot_nt(q[:, sl].astype(BF16), k_ref[:, sl]) * (hd ** -0.5)
        e = jnp.exp(s - jnp.max(s, axis=-1, keepdims=True))
        pr = e / jnp.sum(e, axis=-1, keepdims=True)
        outs.append(_dot(pr.astype(BF16), v_ref[:, sl]))
    o = jnp.concatenate(outs, axis=1).astype(BF16)
    o_ref[0] = x + _dot(o, wo_ref[...])


def _ffn_kernel(x_ref, g_ref, wi_ref, wo_ref, gn_ref, *o_refs, hidden, last):
    x = x_ref[...]
    h = _rms(x, g_ref[...]).astype(BF16)
    acc = x
    for c in range(hidden // FFN_COLS):
        lo = c * FFN_COLS
        gate = _dot(h, wi_ref[:, lo:lo + FFN_COLS])
        up = _dot(h, wi_ref[:, hidden + lo:hidden + lo + FFN_COLS])
        acc = acc + _dot((jax.nn.silu(gate) * up).astype(BF16), wo_ref[lo:lo + FFN_COLS, :])
    normed = _rms(acc, gn_ref[...])
    if last:
        o_refs[0][...] = normed
    else:
        o_refs[0][...] = acc
        o_refs[1][...] = normed.astype(BF16)


def _layer_spec(layer, *tail):
    zeros = (0,) * len(tail)
    return pl.BlockSpec((None,) + tail, lambda *_: (layer,) + zeros)


def _const_spec(*shape):
    zeros = (0,) * len(shape)
    return pl.BlockSpec(shape, lambda *_: zeros)


def _mixer_call(body, name, layer, h, params, scratch, nseq=1):
    bsz, seq, dm = h.shape
    chunk = min(SEQ_CHUNK, seq)
    in_specs = [pl.BlockSpec((nseq, chunk, dm), lambda b, c: (b, c, 0))]
    args = [h]
    for arr, kind in params:
        if kind == "layer":
            in_specs.append(_layer_spec(layer, *arr.shape[1:]))
        elif kind == "seq":
            in_specs.append(pl.BlockSpec((chunk, arr.shape[1]), lambda b, c: (c, 0)))
        else:
            in_specs.append(_const_spec(*arr.shape))
        args.append(arr)
    return pl.pallas_call(
        body,
        grid=(bsz // nseq, seq // chunk),
        in_specs=in_specs,
        out_specs=pl.BlockSpec((nseq, chunk, GROUP), lambda b, c: (b, c, 0)),
        out_shape=jax.ShapeDtypeStruct((bsz, seq, GROUP), BF16),
        scratch_shapes=scratch,
        compiler_params=pltpu.CompilerParams(
            dimension_semantics=("parallel", "arbitrary"), vmem_limit_bytes=VMEM_LIMIT),
        name=name,
    )(*args)


def _head_block_matrix(value):
    idx = jnp.arange(GROUP) // HEAD
    return jnp.where(idx[:, None] == idx[None, :], value, 0.0).astype(BF16)


def kernel(x, mem, norm_mix, w_in, lru_conv_w, lru_conv_b, lru_w_r, lru_b_r, lru_w_i, lru_b_i, lru_lambda, ret_gn_w, ret_gn_b, rwkv_mu, rwkv_w0, rwkv_w2, rwkv_a0, rwkv_a2, rwkv_g2, rwkv_k_k, rwkv_k_a, rwkv_r_k, rwkv_gn_w, rwkv_gn_b, ssd_conv_w, ssd_conv_b, ssd_dt_bias, ssd_a_log, ssd_d, ssd_norm_w, w_out, norm_mem_q, norm_mem_kv, mem_wq, mem_wk, mem_wv, mem_wo, norm_ffn, ffn_w_in, ffn_w_out, norm_final):
    bsz, seq, dm = x.shape
    depth = w_in.shape[0]
    n_mem = mem.shape[1]
    hidden = ffn_w_out.shape[1]
    chunk = min(SEQ_CHUNK, seq)
    tokens = bsz * seq
    row_tile = min(ROW_TILE, tokens)
    q_tile = min(ROW_TILE, seq)
    rwkv_seqs = RWKV_SEQS if bsz % RWKV_SEQS == 0 else 1
    assert seq % chunk == 0 and chunk % RWKV_SUB == 0 and tokens % row_tile == 0
    assert seq % q_tile == 0 and hidden % FFN_COLS == 0

    def vec(a):
        return a.reshape(depth, 1, -1).astype(F32)

    def per_head(a):
        return jnp.repeat(a, HEAD, axis=1).reshape(depth, 1, GROUP).astype(F32)

    c_a, c_b, c_c = 2 * GROUP, 6 * GROUP, 10 * GROUP
    c_dt = c_c + 4 * GROUP
    w_a = w_in[:, :, :c_a].astype(BF16)
    w_b = w_in[:, :, c_a:c_b].astype(BF16)
    w_c = w_in[:, :, c_b:c_c].astype(BF16)
    w_d = w_in[:, :, c_c:c_dt].astype(BF16)
    w_dt = jnp.repeat(w_in[:, :, c_dt:], HEAD, axis=2).astype(BF16)
    eye_h = jnp.eye(NHEAD, dtype=F32)

    def block_diag(w):
        return jnp.einsum("lhij,hg->lhigj", w, eye_h).reshape(depth, GROUP, GROUP)

    lru_wg = jnp.concatenate([block_diag(lru_w_r), block_diag(lru_w_i)], axis=2).astype(BF16)
    lru_bg = jnp.concatenate([lru_b_r.reshape(depth, 1, GROUP), lru_b_i.reshape(depth, 1, GROUP)], axis=2)
    n_lo = rwkv_w2.shape[1]
    zeros_lo = jnp.zeros((depth, n_lo, GROUP), F32)
    rwkv_wa2 = jnp.concatenate([
        jnp.concatenate([rwkv_w2, zeros_lo], axis=2),
        jnp.concatenate([zeros_lo, zeros_lo], axis=2),
        jnp.concatenate([zeros_lo, zeros_lo], axis=2),
        jnp.concatenate([zeros_lo, rwkv_a2], axis=2),
    ], axis=1).astype(BF16)

    avg = _head_block_matrix(1.0 / HEAD)
    ones = _head_block_matrix(1.0)
    t_idx = jnp.arange(chunk)
    tri_full = (t_idx[:, None] >= t_idx[None, :]).astype(BF16)
    tri_sub = ((t_idx[:, None] >= t_idx[None, :])
               & (t_idx[:, None] // RWKV_SUB == t_idx[None, :] // RWKV_SUB)).astype(BF16)
    esel = (jnp.arange(GROUP)[None, :] == (jnp.arange(8 * NHEAD)[:, None] // 8) * HEAD).astype(BF16)

    pos = jnp.arange(seq, dtype=F32)
    inv_freq = ROPE_BASE ** (-jnp.arange(HEAD // 2, dtype=F32) / (HEAD // 2))
    ang = pos[:, None] * inv_freq[None, :]
    cos, sin = jnp.cos(ang), jnp.sin(ang)
    cos_t = jnp.tile(jnp.concatenate([cos, cos], axis=1), (1, NHEAD))
    sin_t = jnp.tile(jnp.concatenate([-sin, sin], axis=1), (1, NHEAD))

    w_out_b = w_out.astype(BF16)
    wq_b, wk_b, wv_b, wo_b = (a.astype(BF16) for a in (mem_wq, mem_wk, mem_wv, mem_wo))
    ffn_wi_b = ffn_w_in.astype(BF16)
    ffn_wo_b = ffn_w_out.astype(BF16)
    g_mix, g_q, g_kv, g_ffn = vec(norm_mix), vec(norm_mem_q), vec(norm_mem_kv), vec(norm_ffn)
    g_next = jnp.concatenate([g_mix[1:], norm_final.reshape(1, 1, dm).astype(F32)], axis=0)
    cparams = pltpu.CompilerParams(dimension_semantics=("parallel",), vmem_limit_bytes=VMEM_LIMIT)
    row = lambda width: pl.BlockSpec((row_tile, width), lambda i: (i, 0))

    kv_shape = jax.ShapeDtypeStruct((depth, bsz, n_mem, dm), BF16)
    kv_spec = pl.BlockSpec((None, None, n_mem, dm), lambda l, b: (l, b, 0, 0))
    w_spec = pl.BlockSpec((None, dm, dm), lambda l, b: (l, 0, 0))
    mem_k, mem_v = pl.pallas_call(
        _kv_kernel,
        grid=(depth, bsz),
        in_specs=[pl.BlockSpec((1, n_mem, dm), lambda l, b: (b, 0, 0)),
                  pl.BlockSpec((None, 1, dm), lambda l, b: (l, 0, 0)), w_spec, w_spec],
        out_specs=(kv_spec, kv_spec),
        out_shape=(kv_shape, kv_shape),
        compiler_params=pltpu.CompilerParams(
            dimension_semantics=("parallel", "parallel"), vmem_limit_bytes=VMEM_LIMIT),
        name="mem_kv",
    )(mem, g_kv, wk_b, wv_b)

    h = pl.pallas_call(
        _prenorm_kernel,
        grid=(tokens // row_tile,),
        in_specs=[row(dm), _layer_spec(0, 1, dm)],
        out_specs=row(dm),
        out_shape=jax.ShapeDtypeStruct((tokens, dm), BF16),
        compiler_params=cparams,
        name="prenorm",
    )(x.reshape(tokens, dm), g_mix).reshape(bsz, seq, dm)

    conv_scr = lambda width: pltpu.VMEM((chunk + CARRY_ROWS, width), F32)
    sq_state = pltpu.VMEM((GROUP, GROUP), F32)

    for l in range(depth):
        lay = lambda a: (a, "layer")
        y_a = _mixer_call(
            _lru_kernel, "mix_lru", l, h,
            [lay(w_a), lay(lru_conv_w), lay(vec(lru_conv_b)), lay(lru_wg), lay(lru_bg),
             lay(vec(lru_lambda))],
            [conv_scr(GROUP), pltpu.VMEM((8, GROUP), F32)])
        y_b = _mixer_call(
            _ret_kernel, "mix_ret", l, h,
            [lay(w_b), (cos_t, "seq"), (sin_t, "seq"), lay(vec(ret_gn_w)),
             lay(vec(ret_gn_b)), (avg, "const")],
            [sq_state, pltpu.VMEM((NHEAD, chunk, chunk), F32), pltpu.VMEM((chunk, GROUP), F32),
             pltpu.VMEM((chunk, GROUP), F32), sq_state])
        y_c = _mixer_call(
            _rwkv_kernel, "mix_rwkv", l, h,
            [lay(w_c), lay(vec(rwkv_mu)), lay(vec(rwkv_w0)), lay(vec(rwkv_a0)),
             lay(rwkv_wa2), lay(rwkv_g2.astype(BF16)), lay(vec(rwkv_k_k)), lay(vec(rwkv_k_a)),
             lay(vec(rwkv_r_k)), lay(vec(rwkv_gn_w)), lay(vec(rwkv_gn_b)), (avg, "const"),
             (ones, "const"), (tri_sub, "const")],
            [pltpu.VMEM((rwkv_seqs, chunk + CARRY_ROWS, 4 * GROUP), F32),
             pltpu.VMEM((rwkv_seqs, GROUP, GROUP), F32)],
            nseq=rwkv_seqs)
        y_d = _mixer_call(
            _ssd_kernel, "mix_ssd", l, h,
            [lay(w_d), lay(w_dt), lay(ssd_conv_w), lay(vec(ssd_conv_b)),
             lay(per_head(ssd_dt_bias)), lay(per_head(ssd_a_log)), lay(per_head(ssd_d)),
             lay(vec(ssd_norm_w)), (tri_full, "const"), (esel, "const")],
            [conv_scr(3 * GROUP), sq_state])

        q_rows = lambda width: pl.BlockSpec((1, q_tile, width), lambda b, i: (b, i, 0))
        kv_l = pl.BlockSpec((None, None, n_mem, dm), lambda b, i, l=l: (l, b, 0, 0))
        x = pl.pallas_call(
            _attn_kernel,
            grid=(bsz, seq // q_tile),
            in_specs=[q_rows(GROUP)] * 4 + [
                q_rows(dm), _layer_spec(l, dm, dm), _layer_spec(l, 1, dm),
                _layer_spec(l, dm, dm), _layer_spec(l, dm, dm), kv_l, kv_l],
            out_specs=q_rows(dm),
            out_shape=jax.ShapeDtypeStruct((bsz, seq, dm), F32),
            compiler_params=pltpu.CompilerParams(
                dimension_semantics=("parallel", "parallel"), vmem_limit_bytes=VMEM_LIMIT),
            name="mem_attn",
        )(y_a, y_b, y_c, y_d, x, w_out_b, g_q, wq_b, wo_b, mem_k, mem_v)

        last = l == depth - 1
        x_shape = jax.ShapeDtypeStruct((tokens, dm), F32)
        outs = pl.pallas_call(
            functools.partial(_ffn_kernel, hidden=hidden, last=last),
            grid=(tokens // row_tile,),
            in_specs=[row(dm), _layer_spec(l, 1, dm),
                      pl.BlockSpec((None, dm, 2 * hidden), lambda i, l=l: (l, 0, 0),
                                   pipeline_mode=pl.Buffered(1)),
                      pl.BlockSpec((None, hidden, dm), lambda i, l=l: (l, 0, 0),
                                   pipeline_mode=pl.Buffered(1)),
                      _layer_spec(l, 1, dm)],
            out_specs=row(dm) if last else (row(dm), row(dm)),
            out_shape=x_shape if last else (x_shape, jax.ShapeDtypeStruct((tokens, dm), BF16)),
            compiler_params=cparams,
            name="ffn",
        )(x.reshape(tokens, dm), g_ffn, ffn_wi_b, ffn_wo_b, g_next)
        if last:
            x = outs.reshape(bsz, seq, dm)
        else:
            x = outs[0].reshape(bsz, seq, dm)
            h = outs[1].reshape(bsz, seq, dm)
    return x
```

```python
import functools
import math

import jax
import jax.numpy as jnp
from jax import lax
from jax.experimental import pallas as pl
from jax.experimental.pallas import tpu as pltpu

F32 = jnp.float32
BF16 = jnp.bfloat16

GROUP = 256
HEAD = 64
NHEAD = GROUP // HEAD
CONV_K = 4
LRU_C = 8.0
GN_EPS = 1e-5
RWKV_GN_EPS = HEAD * 1e-5
NORM_EPS = 1e-6
ROPE_BASE = 10000.0
SSD_NGROUPS = 2
SSD_STATE = 128
MEM_HEADS = 4
LOG_GAMMA = tuple(math.log1p(-(2.0 ** (-5.0 - h))) for h in range(NHEAD))

SEQ_CHUNK = 256
RWKV_SUB = 64
RWKV_SEQS = 4
ROW_TILE = 512
ATTN_TILE = 1024
FFN_COLS = 256
CARRY_ROWS = 8
VMEM_LIMIT = 56 * 1024 * 1024
NEG_BIG = -1e30


def _dot(a, b):
    return jnp.dot(a, b, preferred_element_type=F32)


def _dot_nt(a, b):
    return lax.dot_general(a, b, (((1,), (1,)), ((), ())), preferred_element_type=F32)


def _dot_tn(a, b):
    return lax.dot_general(a, b, (((0,), (0,)), ((), ())), preferred_element_type=F32)


def _split(x, n):
    terms = []
    for _ in range(n):
        t = x.astype(BF16)
        terms.append(t)
        x = x - t.astype(F32)
    return terms


def _cumsum_rows(tri, x):
    return sum(_dot(tri, t) for t in _split(x, 3))


def _rms(x, g):
    ms = jnp.mean(x * x, axis=-1, keepdims=True)
    return (x * lax.rsqrt(ms + NORM_EPS)) * g


def _softplus(x):
    return jnp.maximum(x, 0.0) + jnp.log1p(jnp.exp(-jnp.abs(x)))


def _gelu_tanh(x):
    return 0.5 * x * (1.0 + jnp.tanh(math.sqrt(2.0 / math.pi) * (x + 0.044715 * (x * x * x))))


def _lane_head(rows):
    return lax.broadcasted_iota(jnp.int32, (rows, GROUP), 1) // HEAD


def _stack_heads(z, lane_head):
    return jnp.concatenate(
        [jnp.where(lane_head == h, z, 0.0) for h in range(NHEAD)], axis=0).astype(BF16)


def _head_sum(x, block):
    return _dot(x.astype(BF16), block)


def _causal_conv(buf, cur, cw, cb, rows):
    buf[CARRY_ROWS:CARRY_ROWS + rows, :] = cur
    y = cw[3:4] * cur + cb
    for j in range(CONV_K - 1):
        off = CARRY_ROWS - (CONV_K - 1) + j
        y = y + cw[j:j + 1] * buf[off:off + rows, :]
    buf[0:CARRY_ROWS, :] = buf[rows:rows + CARRY_ROWS, :]
    return y


def _scan_rows(a, u, h0):
    rows, cols = a.shape
    sub = lax.broadcasted_iota(jnp.int32, (8, cols), 0)
    out = []
    h = h0
    for g in range(rows // 8):
        ag = a[g * 8:(g + 1) * 8]
        ug = u[g * 8:(g + 1) * 8]
        for d in (1, 2, 4):
            m = sub >= d
            ug = jnp.where(m, ag * pltpu.roll(ug, d, 0) + ug, ug)
            ag = jnp.where(m, ag * pltpu.roll(ag, d, 0), ag)
        hg = ug + ag * h
        out.append(hg)
        h = hg[7:8]
    return jnp.concatenate(out, axis=0), h


def _group_norm(y, avg, eps, w, b):
    mu = _head_sum(y, avg)
    d = y - mu
    var = _head_sum(d * d, avg)
    return (d * lax.rsqrt(var + eps)) * w + b


def _pipelined_projection(h_ref, hn_ref, pbuf, weights):
    @pl.when(pl.program_id(1) == 0)
    def _():
        off = 0
        for w in weights:
            pbuf[:, off:off + w.shape[1]] = _dot(h_ref[0], w[...])
            off += w.shape[1]

    p = pbuf[...]
    nxt = [_dot(hn_ref[0], w[...]) for w in weights]

    def park():
        off = 0
        for v in nxt:
            pbuf[:, off:off + v.shape[1]] = v
            off += v.shape[1]
    return p, park


def _prenorm_kernel(x_ref, g_ref, o_ref):
    o_ref[...] = _rms(x_ref[...], g_ref[...]).astype(o_ref.dtype)


def _lru_kernel(h_ref, hn_ref, w_ref, cw_ref, cb_ref, wg_ref, bg_ref, lam_ref, o_ref, cbuf, hcar,
                pbuf):
    rows = h_ref.shape[1]

    @pl.when(pl.program_id(1) == 0)
    def _():
        cbuf[0:CARRY_ROWS, :] = jnp.zeros((CARRY_ROWS, GROUP), F32)
        hcar[...] = jnp.zeros_like(hcar)

    p, park = _pipelined_projection(h_ref, hn_ref, pbuf, [w_ref])
    gate = p[:, :GROUP]
    xr = _causal_conv(cbuf, p[:, GROUP:], cw_ref[...], cb_ref[...], rows)
    gates = _dot(xr.astype(BF16), wg_ref[...]) + bg_ref[...]
    r = jax.nn.sigmoid(gates[:, :GROUP])
    i = jax.nn.sigmoid(gates[:, GROUP:])
    log_a = (-LRU_C * r) * _softplus(-lam_ref[...])
    a = jnp.exp(log_a)
    u = jnp.sqrt(-jnp.tanh(log_a) * (1.0 + a * a)) * (i * xr)
    hs, h_last = _scan_rows(a, u, hcar[0:1, :])
    hcar[0:1, :] = h_last
    o_ref[0] = (hs * _gelu_tanh(gate)).astype(o_ref.dtype)
    park()


def _rotary(z, cos, sin_signed):
    lane = lax.broadcasted_iota(jnp.int32, (z.shape[0], 128), 1)
    first = (lane % HEAD) < (HEAD // 2)
    halves = []
    for s in range(2):
        zh = z[:, s * 128:(s + 1) * 128]
        halves.append(jnp.where(first, pltpu.roll(zh, 128 - HEAD // 2, 1),
                                pltpu.roll(zh, HEAD // 2, 1)))
    return z * cos + jnp.concatenate(halves, axis=1) * sin_signed


def _ret_kernel(h_ref, w_ref, cos_ref, sin_ref, gnw_ref, gnb_ref, avg_ref, o_ref,
                state, dmask, qdec, kdec, sdec):
    rows = h_ref.shape[1]
    lane_head = _lane_head(rows)

    @pl.when(pl.program_id(1) == 0)
    def _():
        state[...] = jnp.zeros_like(state)
        t = lax.broadcasted_iota(jnp.int32, (rows, rows), 0)
        s = lax.broadcasted_iota(jnp.int32, (rows, rows), 1)
        dist = (t - s).astype(F32)
        for h in range(NHEAD):
            dmask[h] = jnp.where(t >= s, jnp.exp(dist * LOG_GAMMA[h]), 0.0)
        lg = jnp.zeros((rows, GROUP), F32)
        for h in range(NHEAD):
            lg = jnp.where(lane_head == h, LOG_GAMMA[h], lg)
        tt = lax.broadcasted_iota(jnp.int32, (rows, GROUP), 0).astype(F32)
        qdec[...] = jnp.exp((tt + 1.0) * lg)
        kdec[...] = jnp.exp((rows - 1.0 - tt) * lg)
        row_head = lax.broadcasted_iota(jnp.int32, (GROUP, GROUP), 0) // HEAD
        sd = jnp.zeros((GROUP, GROUP), F32)
        for h in range(NHEAD):
            sd = jnp.where(row_head == h, math.exp(rows * LOG_GAMMA[h]), sd)
        sdec[...] = sd

    p = _dot(h_ref[0], w_ref[...])
    q = _rotary(p[:, 0:GROUP], cos_ref[...], sin_ref[...])
    k = _rotary(p[:, GROUP:2 * GROUP], cos_ref[...], sin_ref[...]) * (HEAD ** -0.5)
    v = p[:, 2 * GROUP:3 * GROUP]
    gate = p[:, 3 * GROUP:]
    kb = k.astype(BF16)
    probs = []
    for hh in range(NHEAD):
        qm = jnp.where(lane_head == hh, q, 0.0).astype(BF16)
        probs.append((_dot_nt(qm, kb) * dmask[hh]).astype(BF16))
    y = _dot(jnp.concatenate(probs, axis=1), _stack_heads(v, lane_head))
    y = y + _dot((q * qdec[...]).astype(BF16), state[...].astype(BF16))
    kv = _dot_tn((k * kdec[...]).astype(BF16), v.astype(BF16))
    row_head = lax.broadcasted_iota(jnp.int32, (GROUP, GROUP), 0) // HEAD
    col_head = lax.broadcasted_iota(jnp.int32, (GROUP, GROUP), 1) // HEAD
    state[...] = state[...] * sdec[...] + jnp.where(row_head == col_head, kv, 0.0)
    yn = _group_norm(y, avg_ref[...], GN_EPS, gnw_ref[...], gnb_ref[...])
    o_ref[0] = (jax.nn.silu(gate) * yn).astype(o_ref.dtype)


def _rwkv_kernel(h_ref, w_ref, mu_ref, w0_ref, a0_ref, wa2_ref, g2_ref, kk_ref, ka_ref,
                 rk_ref, gnw_ref, gnb_ref, avg_ref, ones_ref, tri_ref, o_ref, sbuf, state):
    nseq, rows, dm = h_ref.shape
    sub = RWKV_SUB
    blk = NHEAD * sub
    per_seq = rows // sub
    lane_head = _lane_head(sub)

    @pl.when(pl.program_id(1) == 0)
    def _():
        for b_i in range(nseq):
            sbuf[b_i, 0:CARRY_ROWS, :] = jnp.zeros((CARRY_ROWS, sbuf.shape[2]), F32)
        state[...] = jnp.zeros_like(state)

    p = _dot(h_ref[...].reshape(nseq * rows, dm), w_ref[...])
    shifted = []
    for b_i in range(nseq):
        sbuf[b_i, CARRY_ROWS:CARRY_ROWS + rows, :] = p[b_i * rows:(b_i + 1) * rows]
        shifted.append(sbuf[b_i, CARRY_ROWS - 1:CARRY_ROWS - 1 + rows, :])
        sbuf[b_i, 0:CARRY_ROWS, :] = sbuf[b_i, rows:rows + CARRY_ROWS, :]
    p = p + (jnp.concatenate(shifted, axis=0) - p) * mu_ref[...]
    r = p[:, 0:GROUP]
    k = p[:, GROUP:2 * GROUP]
    v = p[:, 2 * GROUP:3 * GROUP]
    wa = p[:, 3 * GROUP:3 * GROUP + 128]
    g_lo = p[:, 3 * GROUP + 128:]
    lora = _dot(jnp.concatenate([jnp.tanh(wa), wa], axis=1).astype(BF16), wa2_ref[...])
    w_log = -_softplus(-(w0_ref[...] + lora[:, :GROUP])) - 0.5
    log_w = -jnp.exp(w_log)
    a = jax.nn.sigmoid(a0_ref[...] + lora[:, GROUP:])
    gate = _dot(jax.nn.sigmoid(g_lo).astype(BF16), g2_ref[...])
    kk = k * kk_ref[...]
    kk = kk * lax.rsqrt(_head_sum(kk * kk, ones_ref[...]) + 1e-12)
    k = k * (1.0 + (a - 1.0) * ka_ref[...])
    b = kk * a
    cl = jnp.concatenate(
        [_cumsum_rows(tri_ref[...], log_w[b_i * rows:(b_i + 1) * rows]) for b_i in range(nseq)],
        axis=0)

    t_f = lax.broadcasted_iota(jnp.int32, (sub, blk), 0)
    s_f = lax.broadcasted_iota(jnp.int32, (sub, blk), 1) % sub
    strict = t_f > s_f
    incl = t_f >= s_f
    eye = (t_f == s_f).astype(F32)
    r_i = lax.broadcasted_iota(jnp.int32, (blk, blk), 0)
    c_i = lax.broadcasted_iota(jnp.int32, (blk, blk), 1)
    same_head = (r_i // HEAD) == (c_i // HEAD)
    same_blk = (r_i // sub) == (c_i // sub)

    def unfold(f):
        return jnp.where(same_blk, jnp.concatenate([f] * NHEAD, axis=0), 0.0).astype(BF16)

    subs = range(nseq * per_seq)
    kkd, rd, bdl, kdl, e_tot, vs, v_st = [], [], [], [], [], [], []
    a_f, c_cat, ay_cat, cy_cat = [], [], [], []
    for j in subs:
        sl = slice(j * sub, (j + 1) * sub)
        clj = cl[sl]
        tot = clj[sub - 1:sub]
        e_neg = jnp.exp(-clj)
        e_end = jnp.exp(tot - clj)
        rd.append(r[sl] * jnp.exp(clj))
        kkd.append(kk[sl] * jnp.exp(clj - log_w[sl]))
        bd = b[sl] * e_neg
        kd = k[sl] * e_neg
        bdl.append(b[sl] * e_end)
        kdl.append(k[sl] * e_end)
        e_tot.append(jnp.exp(tot))
        vs.append(v[sl])
        v_st.append(_stack_heads(v[sl], lane_head))
        lhs = jnp.concatenate([kkd[j], rd[j]], axis=0).astype(BF16)
        rhs = jnp.concatenate([_stack_heads(bd, lane_head), _stack_heads(kd, lane_head)], axis=0)
        m = _dot_nt(lhs, rhs)
        a_f.append(jnp.where(strict, m[0:sub, 0:blk], 0.0))
        c_cat.append(jnp.where(strict, m[0:sub, blk:], 0.0).astype(BF16))
        ay_cat.append(jnp.where(incl, m[sub:, 0:blk], 0.0).astype(BF16))
        cy_cat.append(jnp.where(incl, m[sub:, blk:], 0.0).astype(BF16))
    t_f32 = [eye - a_ for a_ in a_f]
    pw = a_f
    pw_bd = [unfold(p_) for p_ in pw]
    for _ in range(int(math.log2(sub)) - 1):
        pw = [_dot(pw[j].astype(BF16), pw_bd[j]) for j in subs]
        pw_bd = [unfold(p_) for p_ in pw]
        t_f32 = [t_f32[j] + _dot(t_f32[j].astype(BF16), pw_bd[j]) for j in subs]
    t_cat = [t.astype(BF16) for t in t_f32]
    tk = [_dot(t_cat[j], _stack_heads(kkd[j], lane_head)) for j in subs]
    cv = [_dot(c_cat[j], v_st[j]) for j in subs]
    tcv = [_dot(t_cat[j], _stack_heads(cv[j], lane_head)) for j in subs]
    g_m = [jnp.where(same_head, _dot_tn(tk[j].astype(BF16), bdl[j].astype(BF16)), 0.0).astype(BF16)
           for j in subs]
    h_m = [jnp.where(same_head,
                     _dot_tn(jnp.concatenate([vs[j], -tcv[j]], axis=0).astype(BF16),
                             jnp.concatenate([kdl[j], bdl[j]], axis=0).astype(BF16)), 0.0)
           for j in subs]
    q_t = [(rd[j] - _dot(ay_cat[j], _stack_heads(tk[j], lane_head))).astype(BF16) for j in subs]
    y_0 = [_dot(cy_cat[j], v_st[j]) - _dot(ay_cat[j], _stack_heads(tcv[j], lane_head)) for j in subs]

    s_cur = [state[b_i] for b_i in range(nseq)]
    ys = [None] * len(subs)
    for jj in range(per_seq):
        for b_i in range(nseq):
            j = b_i * per_seq + jj
            s_b = s_cur[b_i].astype(BF16)
            ys[j] = _dot_nt(q_t[j], s_b) + y_0[j]
            s_cur[b_i] = s_cur[b_i] * e_tot[j] - _dot(s_b, g_m[j]) + h_m[j]
    for b_i in range(nseq):
        state[b_i] = s_cur[b_i]
    y = jnp.concatenate(ys, axis=0)
    yn = _group_norm(y, avg_ref[...], RWKV_GN_EPS, gnw_ref[...], gnb_ref[...])
    bonus = _head_sum(r * k * rk_ref[...], ones_ref[...]) * v
    out = ((yn + bonus) * gate).astype(o_ref.dtype)
    for b_i in range(nseq):
        o_ref[b_i] = out[b_i * rows:(b_i + 1) * rows]


def _ssd_kernel(h_ref, hn_ref, w_ref, wdt_ref, cw_ref, cb_ref, dtb_ref, alog_ref, dsk_ref, nw_ref,
                tri_ref, esel_ref, o_ref, cbuf, state, pbuf):
    rows = h_ref.shape[1]
    lane_head = _lane_head(rows)

    @pl.when(pl.program_id(1) == 0)
    def _():
        cbuf[0:CARRY_ROWS, :] = jnp.zeros((CARRY_ROWS, cbuf.shape[1]), F32)
        state[...] = jnp.zeros_like(state)

    p, park = _pipelined_projection(h_ref, hn_ref, pbuf, [w_ref, wdt_ref])
    z = p[:, :GROUP]
    xbc = jax.nn.silu(_causal_conv(cbuf, p[:, GROUP:4 * GROUP], cw_ref[...], cb_ref[...], rows))
    xs = xbc[:, :GROUP]
    bm = xbc[:, GROUP:2 * GROUP].astype(BF16)
    cm = xbc[:, 2 * GROUP:].astype(BF16)
    dt = _softplus(p[:, 4 * GROUP:] + dtb_ref[...])
    log_a = dt * (-jnp.exp(alog_ref[...]))
    cum = _cumsum_rows(tri_ref[...], log_a)
    tot = cum[rows - 1:rows]
    vdt = xs * dt

    cum_rows = sum(_dot_nt(esel_ref[...], t) for t in _split(cum, 3))
    lane = lax.broadcasted_iota(jnp.int32, (rows, 128), 1)
    low = lane < HEAD
    cols = []
    for s in range(2):
        ch = cum[:, s * 128:(s + 1) * 128]
        sw = pltpu.roll(ch, HEAD, 1)
        cols.append(jnp.where(low, ch, sw))
        cols.append(jnp.where(low, sw, ch))
    t_i = lax.broadcasted_iota(jnp.int32, (rows, rows), 0)
    s_i = lax.broadcasted_iota(jnp.int32, (rows, rows), 1)
    causal = t_i >= s_i
    scores = [_dot_nt(cm[:, g * SSD_STATE:(g + 1) * SSD_STATE], bm[:, g * SSD_STATE:(g + 1) * SSD_STATE])
              for g in range(SSD_NGROUPS)]
    probs = []
    for hh in range(NHEAD):
        col = jnp.concatenate([cols[hh]] * (rows // 128), axis=1)
        row = jnp.broadcast_to(cum_rows[8 * hh:8 * hh + 1, :], (rows, rows))
        decay = jnp.exp(jnp.where(causal, col - row, NEG_BIG))
        probs.append((scores[hh // (NHEAD // SSD_NGROUPS)] * decay).astype(BF16))
    y = _dot(jnp.concatenate(probs, axis=1), _stack_heads(vdt, lane_head))
    y = y + _dot(cm, state[...].astype(BF16)) * jnp.exp(cum)
    upd = _dot_tn(bm, (vdt * jnp.exp(tot - cum)).astype(BF16))
    r_g = lax.broadcasted_iota(jnp.int32, (GROUP, GROUP), 0) // SSD_STATE
    c_g = lax.broadcasted_iota(jnp.int32, (GROUP, GROUP), 1) // SSD_STATE
    state[...] = state[...] * jnp.exp(tot) + jnp.where(r_g == c_g, upd, 0.0)
    y = (y + dsk_ref[...] * xs) * jax.nn.silu(z)
    outs = []
    for g in range(SSD_NGROUPS):
        yg = y[:, g * 128:(g + 1) * 128]
        outs.append(yg * lax.rsqrt(jnp.mean(yg * yg, axis=-1, keepdims=True) + NORM_EPS))
    o_ref[0] = (jnp.concatenate(outs, axis=1) * nw_ref[...]).astype(o_ref.dtype)
    park()


def _kv_kernel(m_ref, g_ref, wk_ref, wv_ref, k_ref, v_ref):
    h = _rms(m_ref[...], g_ref[...]).astype(BF16)
    k_ref[...] = _dot(h, wk_ref[...]).astype(k_ref.dtype)
    v_ref[...] = _dot(h, wv_ref[...]).astype(v_ref.dtype)


def _attn_kernel(ya_ref, yb_ref, yc_ref, yd_ref, x_ref, wmix_ref, g_ref, wq_ref, wo_ref,
                 k_ref, v_ref, o_ref):
    y = jnp.concatenate([ya_ref[0], yb_ref[0], yc_ref[0], yd_ref[0]], axis=1)
    x = x_ref[0] + _dot(y, wmix_ref[...])
    hd = x.shape[1] // MEM_HEADS
    q = _dot(_rms(x, g_ref[...]).astype(BF16), wq_ref[...])
    heads = [slice(hh * hd, (hh + 1) * hd) for hh in range(MEM_HEADS)]
    scores = [_dot_nt(q[:, sl].astype(BF16), k_ref[:, sl]) * (hd ** -0.5) for sl in heads]
    outs = []
    for s, sl in zip(scores, heads):
        e = jnp.exp(s - jnp.max(s, axis=-1, keepdims=True))
        pr = e / jnp.sum(e, axis=-1, keepdims=True)
        outs.append(_dot(pr.astype(BF16), v_ref[:, sl]))
    o = jnp.concatenate(outs, axis=1).astype(BF16)
    o_ref[0] = x + _dot(o, wo_ref[...])


def _ffn_kernel(x_ref, g_ref, wi_ref, wo_ref, gn_ref, *o_refs, hidden, last):
    x = x_ref[...]
    h = _rms(x, g_ref[...]).astype(BF16)
    acc = x
    for c in range(hidden // FFN_COLS):
        lo = c * FFN_COLS
        gate = _dot(h, wi_ref[:, lo:lo + FFN_COLS])
        up = _dot(h, wi_ref[:, hidden + lo:hidden + lo + FFN_COLS])
        acc = acc + _dot((jax.nn.silu(gate) * up).astype(BF16), wo_ref[lo:lo + FFN_COLS, :])
    normed = _rms(acc, gn_ref[...])
    if last:
        o_refs[0][...] = normed
    else:
        o_refs[0][...] = acc
        o_refs[1][...] = normed.astype(BF16)


def _layer_spec(layer, *tail):
    zeros = (0,) * len(tail)
    return pl.BlockSpec((None,) + tail, lambda *_: (layer,) + zeros)


def _const_spec(*shape):
    zeros = (0,) * len(shape)
    return pl.BlockSpec(shape, lambda *_: zeros)


def _mixer_call(body, name, layer, h, params, scratch, nseq=1, lookahead=False):
    bsz, seq, dm = h.shape
    chunk = min(SEQ_CHUNK, seq)
    in_specs = [pl.BlockSpec((nseq, chunk, dm), lambda b, c: (b, c, 0))]
    args = [h]
    if lookahead:
        last = seq // chunk - 1
        in_specs.append(
            pl.BlockSpec((nseq, chunk, dm), lambda b, c: (b, jnp.minimum(c + 1, last), 0)))
        args.append(h)
    for arr, kind in params:
        if kind == "layer":
            in_specs.append(_layer_spec(layer, *arr.shape[1:]))
        elif kind == "seq":
            in_specs.append(pl.BlockSpec((chunk, arr.shape[1]), lambda b, c: (c, 0)))
        else:
            in_specs.append(_const_spec(*arr.shape))
        args.append(arr)
    return pl.pallas_call(
        body,
        grid=(bsz // nseq, seq // chunk),
        in_specs=in_specs,
        out_specs=pl.BlockSpec((nseq, chunk, GROUP), lambda b, c: (b, c, 0)),
        out_shape=jax.ShapeDtypeStruct((bsz, seq, GROUP), BF16),
        scratch_shapes=scratch,
        compiler_params=pltpu.CompilerParams(
            dimension_semantics=("parallel", "arbitrary"), vmem_limit_bytes=VMEM_LIMIT),
        name=name,
    )(*args)


def _head_block_matrix(value):
    idx = jnp.arange(GROUP) // HEAD
    return jnp.where(idx[:, None] == idx[None, :], value, 0.0).astype(BF16)


def kernel(x, mem, norm_mix, w_in, lru_conv_w, lru_conv_b, lru_w_r, lru_b_r, lru_w_i, lru_b_i, lru_lambda, ret_gn_w, ret_gn_b, rwkv_mu, rwkv_w0, rwkv_w2, rwkv_a0, rwkv_a2, rwkv_g2, rwkv_k_k, rwkv_k_a, rwkv_r_k, rwkv_gn_w, rwkv_gn_b, ssd_conv_w, ssd_conv_b, ssd_dt_bias, ssd_a_log, ssd_d, ssd_norm_w, w_out, norm_mem_q, norm_mem_kv, mem_wq, mem_wk, mem_wv, mem_wo, norm_ffn, ffn_w_in, ffn_w_out, norm_final):
    bsz, seq, dm = x.shape
    depth = w_in.shape[0]
    n_mem = mem.shape[1]
    hidden = ffn_w_out.shape[1]
    chunk = min(SEQ_CHUNK, seq)
    tokens = bsz * seq
    row_tile = min(ROW_TILE, tokens)
    q_tile = min(ATTN_TILE, seq)
    rwkv_seqs = RWKV_SEQS if bsz % RWKV_SEQS == 0 else 1
    assert seq % chunk == 0 and chunk % RWKV_SUB == 0 and tokens % row_tile == 0
    assert seq % q_tile == 0 and hidden % FFN_COLS == 0

    def vec(a):
        return a.reshape(depth, 1, -1).astype(F32)

    def per_head(a):
        return jnp.repeat(a, HEAD, axis=1).reshape(depth, 1, GROUP).astype(F32)

    c_a, c_b, c_c = 2 * GROUP, 6 * GROUP, 10 * GROUP
    c_dt = c_c + 4 * GROUP
    w_a = w_in[:, :, :c_a].astype(BF16)
    w_b = w_in[:, :, c_a:c_b].astype(BF16)
    w_c = w_in[:, :, c_b:c_c].astype(BF16)
    w_d = w_in[:, :, c_c:c_dt].astype(BF16)
    w_dt = jnp.repeat(w_in[:, :, c_dt:], HEAD, axis=2).astype(BF16)
    eye_h = jnp.eye(NHEAD, dtype=F32)

    def block_diag(w):
        return jnp.einsum("lhij,hg->lhigj", w, eye_h).reshape(depth, GROUP, GROUP)

    lru_wg = jnp.concatenate([block_diag(lru_w_r), block_diag(lru_w_i)], axis=2).astype(BF16)
    lru_bg = jnp.concatenate([lru_b_r.reshape(depth, 1, GROUP), lru_b_i.reshape(depth, 1, GROUP)], axis=2)
    n_lo = rwkv_w2.shape[1]
    zeros_lo = jnp.zeros((depth, n_lo, GROUP), F32)
    rwkv_wa2 = jnp.concatenate([
        jnp.concatenate([rwkv_w2, zeros_lo], axis=2),
        jnp.concatenate([zeros_lo, zeros_lo], axis=2),
        jnp.concatenate([zeros_lo, zeros_lo], axis=2),
        jnp.concatenate([zeros_lo, rwkv_a2], axis=2),
    ], axis=1).astype(BF16)

    avg = _head_block_matrix(1.0 / HEAD)
    ones = _head_block_matrix(1.0)
    t_idx = jnp.arange(chunk)
    tri_full = (t_idx[:, None] >= t_idx[None, :]).astype(BF16)
    tri_sub = ((t_idx[:, None] >= t_idx[None, :])
               & (t_idx[:, None] // RWKV_SUB == t_idx[None, :] // RWKV_SUB)).astype(BF16)
    esel = (jnp.arange(GROUP)[None, :] == (jnp.arange(8 * NHEAD)[:, None] // 8) * HEAD).astype(BF16)

    pos = jnp.arange(seq, dtype=F32)
    inv_freq = ROPE_BASE ** (-jnp.arange(HEAD // 2, dtype=F32) / (HEAD // 2))
    ang = pos[:, None] * inv_freq[None, :]
    cos, sin = jnp.cos(ang), jnp.sin(ang)
    cos_t = jnp.tile(jnp.concatenate([cos, cos], axis=1), (1, NHEAD))
    sin_t = jnp.tile(jnp.concatenate([-sin, sin], axis=1), (1, NHEAD))

    w_out_b = w_out.astype(BF16)
    wq_b, wk_b, wv_b, wo_b = (a.astype(BF16) for a in (mem_wq, mem_wk, mem_wv, mem_wo))
    ffn_wi_b = ffn_w_in.astype(BF16)
    ffn_wo_b = ffn_w_out.astype(BF16)
    g_mix, g_q, g_kv, g_ffn = vec(norm_mix), vec(norm_mem_q), vec(norm_mem_kv), vec(norm_ffn)
    g_next = jnp.concatenate([g_mix[1:], norm_final.reshape(1, 1, dm).astype(F32)], axis=0)
    cparams = pltpu.CompilerParams(dimension_semantics=("parallel",), vmem_limit_bytes=VMEM_LIMIT)
    row = lambda width: pl.BlockSpec((row_tile, width), lambda i: (i, 0))

    mem_rows = bsz * n_mem
    kv_tile = min(ATTN_TILE, mem_rows)
    assert mem_rows % kv_tile == 0
    kv_shape = jax.ShapeDtypeStruct((depth, mem_rows, dm), BF16)
    kv_spec = pl.BlockSpec((None, kv_tile, dm), lambda l, i: (l, i, 0))
    w_spec = pl.BlockSpec((None, dm, dm), lambda l, i: (l, 0, 0))
    mem_k, mem_v = pl.pallas_call(
        _kv_kernel,
        grid=(depth, mem_rows // kv_tile),
        in_specs=[pl.BlockSpec((kv_tile, dm), lambda l, i: (i, 0)),
                  pl.BlockSpec((None, 1, dm), lambda l, i: (l, 0, 0)), w_spec, w_spec],
        out_specs=(kv_spec, kv_spec),
        out_shape=(kv_shape, kv_shape),
        compiler_params=pltpu.CompilerParams(
            dimension_semantics=("parallel", "parallel"), vmem_limit_bytes=VMEM_LIMIT),
        name="mem_kv",
    )(mem.reshape(mem_rows, dm), g_kv, wk_b, wv_b)
    mem_k = mem_k.reshape(depth, bsz, n_mem, dm)
    mem_v = mem_v.reshape(depth, bsz, n_mem, dm)

    h = pl.pallas_call(
        _prenorm_kernel,
        grid=(tokens // row_tile,),
        in_specs=[row(dm), _layer_spec(0, 1, dm)],
        out_specs=row(dm),
        out_shape=jax.ShapeDtypeStruct((tokens, dm), BF16),
        compiler_params=cparams,
        name="prenorm",
    )(x.reshape(tokens, dm), g_mix).reshape(bsz, seq, dm)

    conv_scr = lambda width: pltpu.VMEM((chunk + CARRY_ROWS, width), F32)
    sq_state = pltpu.VMEM((GROUP, GROUP), F32)

    for l in range(depth):
        lay = lambda a: (a, "layer")
        y_a = _mixer_call(
            _lru_kernel, "mix_lru", l, h,
            [lay(w_a), lay(lru_conv_w), lay(vec(lru_conv_b)), lay(lru_wg), lay(lru_bg),
             lay(vec(lru_lambda))],
            [conv_scr(GROUP), pltpu.VMEM((8, GROUP), F32), pltpu.VMEM((chunk, 2 * GROUP), F32)],
            lookahead=True)
        y_b = _mixer_call(
            _ret_kernel, "mix_ret", l, h,
            [lay(w_b), (cos_t, "seq"), (sin_t, "seq"), lay(vec(ret_gn_w)),
             lay(vec(ret_gn_b)), (avg, "const")],
            [sq_state, pltpu.VMEM((NHEAD, chunk, chunk), F32), pltpu.VMEM((chunk, GROUP), F32),
             pltpu.VMEM((chunk, GROUP), F32), sq_state])
        y_c = _mixer_call(
            _rwkv_kernel, "mix_rwkv", l, h,
            [lay(w_c), lay(vec(rwkv_mu)), lay(vec(rwkv_w0)), lay(vec(rwkv_a0)),
             lay(rwkv_wa2), lay(rwkv_g2.astype(BF16)), lay(vec(rwkv_k_k)), lay(vec(rwkv_k_a)),
             lay(vec(rwkv_r_k)), lay(vec(rwkv_gn_w)), lay(vec(rwkv_gn_b)), (avg, "const"),
             (ones, "const"), (tri_sub, "const")],
            [pltpu.VMEM((rwkv_seqs, chunk + CARRY_ROWS, 4 * GROUP), F32),
             pltpu.VMEM((rwkv_seqs, GROUP, GROUP), F32)],
            nseq=rwkv_seqs)
        y_d = _mixer_call(
            _ssd_kernel, "mix_ssd", l, h,
            [lay(w_d), lay(w_dt), lay(ssd_conv_w), lay(vec(ssd_conv_b)),
             lay(per_head(ssd_dt_bias)), lay(per_head(ssd_a_log)), lay(per_head(ssd_d)),
             lay(vec(ssd_norm_w)), (tri_full, "const"), (esel, "const")],
            [conv_scr(3 * GROUP), sq_state, pltpu.VMEM((chunk, 5 * GROUP), F32)], lookahead=True)

        q_rows = lambda width: pl.BlockSpec((1, q_tile, width), lambda b, i: (b, i, 0))
        kv_l = pl.BlockSpec((None, None, n_mem, dm), lambda b, i, l=l: (l, b, 0, 0))
        x = pl.pallas_call(
            _attn_kernel,
            grid=(bsz, seq // q_tile),
            in_specs=[q_rows(GROUP)] * 4 + [
                q_rows(dm), _layer_spec(l, dm, dm), _layer_spec(l, 1, dm),
                _layer_spec(l, dm, dm), _layer_spec(l, dm, dm), kv_l, kv_l],
            out_specs=q_rows(dm),
            out_shape=jax.ShapeDtypeStruct((bsz, seq, dm), F32),
            compiler_params=pltpu.CompilerParams(
                dimension_semantics=("parallel", "parallel"), vmem_limit_bytes=VMEM_LIMIT),
            name="mem_attn",
        )(y_a, y_b, y_c, y_d, x, w_out_b, g_q, wq_b, wo_b, mem_k, mem_v)

        last = l == depth - 1
        x_shape = jax.ShapeDtypeStruct((tokens, dm), F32)
        outs = pl.pallas_call(
            functools.partial(_ffn_kernel, hidden=hidden, last=last),
            grid=(tokens // row_tile,),
            in_specs=[row(dm), _layer_spec(l, 1, dm),
                      pl.BlockSpec((None, dm, 2 * hidden), lambda i, l=l: (l, 0, 0),
                                   pipeline_mode=pl.Buffered(1)),
                      pl.BlockSpec((None, hidden, dm), lambda i, l=l: (l, 0, 0),
                                   pipeline_mode=pl.Buffered(1)),
                      _layer_spec(l, 1, dm)],
            out_specs=row(dm) if last else (row(dm), row(dm)),
            out_shape=x_shape if last else (x_shape, jax.ShapeDtypeStruct((tokens, dm), BF16)),
            compiler_params=cparams,
            name="ffn",
        )(x.reshape(tokens, dm), g_ffn, ffn_wi_b, ffn_wo_b, g_next)
        if last:
            x = outs.reshape(bsz, seq, dm)
        else:
            x = outs[0].reshape(bsz, seq, dm)
            h = outs[1].reshape(bsz, seq, dm)
    return x
```

```python
import functools
import math

import jax
import jax.numpy as jnp
from jax import lax
from jax.experimental import pallas as pl
from jax.experimental.pallas import tpu as pltpu

F32 = jnp.float32
BF16 = jnp.bfloat16

GROUP = 256
HEAD = 64
NHEAD = GROUP // HEAD
CONV_K = 4
LRU_C = 8.0
GN_EPS = 1e-5
RWKV_GN_EPS = HEAD * 1e-5
NORM_EPS = 1e-6
ROPE_BASE = 10000.0
SSD_NGROUPS = 2
SSD_STATE = 128
MEM_HEADS = 4
LOG_GAMMA = tuple(math.log1p(-(2.0 ** (-5.0 - h))) for h in range(NHEAD))

SEQ_CHUNK = 256
RWKV_SUB = 64
RWKV_SEQS = 4
ROW_TILE = 512
ATTN_TILE = 1024
FFN_COLS = 256
CARRY_ROWS = 8
VMEM_LIMIT = 56 * 1024 * 1024
NEG_BIG = -1e30


def _dot(a, b):
    return jnp.dot(a, b, preferred_element_type=F32)


def _dot_nt(a, b):
    return lax.dot_general(a, b, (((1,), (1,)), ((), ())), preferred_element_type=F32)


def _dot_tn(a, b):
    return lax.dot_general(a, b, (((0,), (0,)), ((), ())), preferred_element_type=F32)


def _split(x, n):
    terms = []
    for _ in range(n):
        t = x.astype(BF16)
        terms.append(t)
        x = x - t.astype(F32)
    return terms


def _cumsum_rows(tri, x):
    return sum(_dot(tri, t) for t in _split(x, 3))


def _rms(x, g):
    ms = jnp.mean(x * x, axis=-1, keepdims=True)
    return (x * lax.rsqrt(ms + NORM_EPS)) * g


def _softplus(x):
    return jnp.maximum(x, 0.0) + jnp.log1p(jnp.exp(-jnp.abs(x)))


def _gelu_tanh(x):
    return 0.5 * x * (1.0 + jnp.tanh(math.sqrt(2.0 / math.pi) * (x + 0.044715 * (x * x * x))))


def _lane_head(rows):
    return lax.broadcasted_iota(jnp.int32, (rows, GROUP), 1) // HEAD


def _stack_heads(z, lane_head):
    return jnp.concatenate(
        [jnp.where(lane_head == h, z, 0.0) for h in range(NHEAD)], axis=0).astype(BF16)


def _head_sum(x, block):
    return _dot(x.astype(BF16), block)


def _causal_conv(buf, cur, cw, cb, rows):
    assert cw.shape[0] == CONV_K == 4
    xb, bb = buf.at[0], buf.at[1]
    xb[CARRY_ROWS:CARRY_ROWS + rows, :] = cur
    prev = xb[CARRY_ROWS - 1:CARRY_ROWS - 1 + rows, :]
    bb[CARRY_ROWS:CARRY_ROWS + rows, :] = cw[1:2] * cur + cw[0:1] * prev
    y = (cw[3:4] * cur + cb) + cw[2:3] * prev + bb[CARRY_ROWS - 2:CARRY_ROWS - 2 + rows, :]
    xb[0:CARRY_ROWS, :] = xb[rows:rows + CARRY_ROWS, :]
    bb[0:CARRY_ROWS, :] = bb[rows:rows + CARRY_ROWS, :]
    return y


def _scan_rows(a, u, h0):
    rows, cols = a.shape
    sub = lax.broadcasted_iota(jnp.int32, (8, cols), 0)
    out = []
    h = h0
    for g in range(rows // 8):
        ag = a[g * 8:(g + 1) * 8]
        ug = u[g * 8:(g + 1) * 8]
        for d in (1, 2, 4):
            m = sub >= d
            ug = jnp.where(m, ag * pltpu.roll(ug, d, 0) + ug, ug)
            ag = jnp.where(m, ag * pltpu.roll(ag, d, 0), ag)
        hg = ug + ag * h
        out.append(hg)
        h = hg[7:8]
    return jnp.concatenate(out, axis=0), h


def _group_norm(y, avg, eps, w, b):
    mu = _head_sum(y, avg)
    d = y - mu
    var = _head_sum(d * d, avg)
    return (d * lax.rsqrt(var + eps)) * w + b


def _pipelined_projection(h_ref, hn_ref, pbuf, weights):
    @pl.when(pl.program_id(1) == 0)
    def _():
        off = 0
        for w in weights:
            pbuf[:, off:off + w.shape[1]] = _dot(h_ref[0], w[...])
            off += w.shape[1]

    p = pbuf[...]
    nxt = [_dot(hn_ref[0], w[...]) for w in weights]

    def park():
        off = 0
        for v in nxt:
            pbuf[:, off:off + v.shape[1]] = v
            off += v.shape[1]
    return p, park


def _prenorm_kernel(x_ref, g_ref, o_ref):
    o_ref[...] = _rms(x_ref[...], g_ref[...]).astype(o_ref.dtype)


def _lru_kernel(h_ref, w_ref, cw_ref, cb_ref, wg_ref, bg_ref, lam_ref, o_ref, cbuf, hcar):
    rows = h_ref.shape[1]

    @pl.when(pl.program_id(1) == 0)
    def _():
        cbuf[:, 0:CARRY_ROWS, :] = jnp.zeros((2, CARRY_ROWS, GROUP), F32)
        hcar[...] = jnp.zeros_like(hcar)

    p = _dot(h_ref[0], w_ref[...])
    gate = p[:, :GROUP]
    xr = _causal_conv(cbuf, p[:, GROUP:], cw_ref[...], cb_ref[...], rows)
    gates = _dot(xr.astype(BF16), wg_ref[...]) + bg_ref[...]
    r = jax.nn.sigmoid(gates[:, :GROUP])
    i = jax.nn.sigmoid(gates[:, GROUP:])
    log_a = (-LRU_C * r) * _softplus(-lam_ref[...])
    a = jnp.exp(log_a)
    z = -jnp.tanh(log_a) * (1.0 + a * a)
    u = jnp.where(z > 0.0, z * lax.rsqrt(z), 0.0) * (i * xr)
    hs, h_last = _scan_rows(a, u, hcar[0:1, :])
    hcar[0:1, :] = h_last
    o_ref[0] = (hs * _gelu_tanh(gate)).astype(o_ref.dtype)


def _rotary(z, cos, sin_signed):
    lane = lax.broadcasted_iota(jnp.int32, (z.shape[0], 128), 1)
    first = (lane % HEAD) < (HEAD // 2)
    halves = []
    for s in range(2):
        zh = z[:, s * 128:(s + 1) * 128]
        halves.append(jnp.where(first, pltpu.roll(zh, 128 - HEAD // 2, 1),
                                pltpu.roll(zh, HEAD // 2, 1)))
    return z * cos + jnp.concatenate(halves, axis=1) * sin_signed


def _ret_kernel(h_ref, w_ref, cos_ref, sin_ref, gnw_ref, gnb_ref, avg_ref, o_ref,
                state, dmask, qdec, kdec, sdec):
    rows = h_ref.shape[1]
    lane_head = _lane_head(rows)

    @pl.when(pl.program_id(1) == 0)
    def _():
        state[...] = jnp.zeros_like(state)
        t = lax.broadcasted_iota(jnp.int32, (rows, rows), 0)
        s = lax.broadcasted_iota(jnp.int32, (rows, rows), 1)
        dist = (t - s).astype(F32)
        for h in range(NHEAD):
            dmask[h] = jnp.where(t >= s, jnp.exp(dist * LOG_GAMMA[h]), 0.0)
        lg = jnp.zeros((rows, GROUP), F32)
        for h in range(NHEAD):
            lg = jnp.where(lane_head == h, LOG_GAMMA[h], lg)
        tt = lax.broadcasted_iota(jnp.int32, (rows, GROUP), 0).astype(F32)
        qdec[...] = jnp.exp((tt + 1.0) * lg)
        kdec[...] = jnp.exp((rows - 1.0 - tt) * lg)
        row_head = lax.broadcasted_iota(jnp.int32, (GROUP, GROUP), 0) // HEAD
        sd = jnp.zeros((GROUP, GROUP), F32)
        for h in range(NHEAD):
            sd = jnp.where(row_head == h, math.exp(rows * LOG_GAMMA[h]), sd)
        sdec[...] = sd

    p = _dot(h_ref[0], w_ref[...])
    q = _rotary(p[:, 0:GROUP], cos_ref[...], sin_ref[...])
    k = _rotary(p[:, GROUP:2 * GROUP], cos_ref[...], sin_ref[...]) * (HEAD ** -0.5)
    v = p[:, 2 * GROUP:3 * GROUP]
    gate = p[:, 3 * GROUP:]
    kb = k.astype(BF16)
    probs = []
    for hh in range(NHEAD):
        qm = jnp.where(lane_head == hh, q, 0.0).astype(BF16)
        probs.append((_dot_nt(qm, kb) * dmask[hh]).astype(BF16))
    y = _dot(jnp.concatenate(probs, axis=1), _stack_heads(v, lane_head))
    y = y + _dot((q * qdec[...]).astype(BF16), state[...].astype(BF16))
    kv = _dot_tn((k * kdec[...]).astype(BF16), v.astype(BF16))
    row_head = lax.broadcasted_iota(jnp.int32, (GROUP, GROUP), 0) // HEAD
    col_head = lax.broadcasted_iota(jnp.int32, (GROUP, GROUP), 1) // HEAD
    state[...] = state[...] * sdec[...] + jnp.where(row_head == col_head, kv, 0.0)
    yn = _group_norm(y, avg_ref[...], GN_EPS, gnw_ref[...], gnb_ref[...])
    o_ref[0] = (jax.nn.silu(gate) * yn).astype(o_ref.dtype)


def _rwkv_kernel(h_ref, w_ref, mu_ref, w0_ref, a0_ref, wa2_ref, g2_ref, kk_ref, ka_ref,
                 rk_ref, gnw_ref, gnb_ref, avg_ref, ones_ref, tri_ref, o_ref, sbuf, state):
    nseq, rows, dm = h_ref.shape
    sub = RWKV_SUB
    blk = NHEAD * sub
    per_seq = rows // sub
    lane_head = _lane_head(sub)

    @pl.when(pl.program_id(1) == 0)
    def _():
        for b_i in range(nseq):
            sbuf[b_i, 0:CARRY_ROWS, :] = jnp.zeros((CARRY_ROWS, sbuf.shape[2]), F32)
        state[...] = jnp.zeros_like(state)

    p = _dot(h_ref[...].reshape(nseq * rows, dm), w_ref[...])
    shifted = []
    for b_i in range(nseq):
        sbuf[b_i, CARRY_ROWS:CARRY_ROWS + rows, :] = p[b_i * rows:(b_i + 1) * rows]
        shifted.append(sbuf[b_i, CARRY_ROWS - 1:CARRY_ROWS - 1 + rows, :])
        sbuf[b_i, 0:CARRY_ROWS, :] = sbuf[b_i, rows:rows + CARRY_ROWS, :]
    p = p + (jnp.concatenate(shifted, axis=0) - p) * mu_ref[...]
    r = p[:, 0:GROUP]
    k = p[:, GROUP:2 * GROUP]
    v = p[:, 2 * GROUP:3 * GROUP]
    wa = p[:, 3 * GROUP:3 * GROUP + 128]
    g_lo = p[:, 3 * GROUP + 128:]
    lora = _dot(jnp.concatenate([jnp.tanh(wa), wa], axis=1).astype(BF16), wa2_ref[...])
    w_log = -_softplus(-(w0_ref[...] + lora[:, :GROUP])) - 0.5
    log_w = -jnp.exp(w_log)
    a = jax.nn.sigmoid(a0_ref[...] + lora[:, GROUP:])
    gate = _dot(jax.nn.sigmoid(g_lo).astype(BF16), g2_ref[...])
    kk = k * kk_ref[...]
    kk = kk * lax.rsqrt(_head_sum(kk * kk, ones_ref[...]) + 1e-12)
    k = k * (1.0 + (a - 1.0) * ka_ref[...])
    b = kk * a
    cl = jnp.concatenate(
        [_cumsum_rows(tri_ref[...], log_w[b_i * rows:(b_i + 1) * rows]) for b_i in range(nseq)],
        axis=0)

    t_f = lax.broadcasted_iota(jnp.int32, (sub, blk), 0)
    s_f = lax.broadcasted_iota(jnp.int32, (sub, blk), 1) % sub
    strict = t_f > s_f
    incl = t_f >= s_f
    eye = (t_f == s_f).astype(F32)
    r_i = lax.broadcasted_iota(jnp.int32, (blk, blk), 0)
    c_i = lax.broadcasted_iota(jnp.int32, (blk, blk), 1)
    same_head = (r_i // HEAD) == (c_i // HEAD)
    same_blk = (r_i // sub) == (c_i // sub)

    def unfold(f):
        return jnp.where(same_blk, jnp.concatenate([f] * NHEAD, axis=0), 0.0).astype(BF16)

    subs = range(nseq * per_seq)
    kkd, rd, bdl, kdl, e_tot, vs, v_st = [], [], [], [], [], [], []
    a_f, c_cat, ay_cat, cy_cat = [], [], [], []
    for j in subs:
        sl = slice(j * sub, (j + 1) * sub)
        clj = cl[sl]
        tot = clj[sub - 1:sub]
        e_neg = jnp.exp(-clj)
        e_end = jnp.exp(tot - clj)
        rd.append(r[sl] * jnp.exp(clj))
        kkd.append(kk[sl] * jnp.exp(clj - log_w[sl]))
        bd = b[sl] * e_neg
        kd = k[sl] * e_neg
        bdl.append(b[sl] * e_end)
        kdl.append(k[sl] * e_end)
        e_tot.append(jnp.exp(tot))
        vs.append(v[sl])
        v_st.append(_stack_heads(v[sl], lane_head))
        lhs = jnp.concatenate([kkd[j], rd[j]], axis=0).astype(BF16)
        rhs = jnp.concatenate([_stack_heads(bd, lane_head), _stack_heads(kd, lane_head)], axis=0)
        m = _dot_nt(lhs, rhs)
        a_f.append(jnp.where(strict, m[0:sub, 0:blk], 0.0))
        c_cat.append(jnp.where(strict, m[0:sub, blk:], 0.0).astype(BF16))
        ay_cat.append(jnp.where(incl, m[sub:, 0:blk], 0.0).astype(BF16))
        cy_cat.append(jnp.where(incl, m[sub:, blk:], 0.0).astype(BF16))
    t_f32 = [eye - a_ for a_ in a_f]
    pw = a_f
    pw_bd = [unfold(p_) for p_ in pw]
    for _ in range(int(math.log2(sub)) - 1):
        pw = [_dot(pw[j].astype(BF16), pw_bd[j]) for j in subs]
        pw_bd = [unfold(p_) for p_ in pw]
        t_f32 = [t_f32[j] + _dot(t_f32[j].astype(BF16), pw_bd[j]) for j in subs]
    t_cat = [t.astype(BF16) for t in t_f32]
    tk = [_dot(t_cat[j], _stack_heads(kkd[j], lane_head)) for j in subs]
    cv = [_dot(c_cat[j], v_st[j]) for j in subs]
    tcv = [_dot(t_cat[j], _stack_heads(cv[j], lane_head)) for j in subs]
    g_m = [jnp.where(same_head, _dot_tn(tk[j].astype(BF16), bdl[j].astype(BF16)), 0.0).astype(BF16)
           for j in subs]
    h_m = [jnp.where(same_head,
                     _dot_tn(jnp.concatenate([vs[j], -tcv[j]], axis=0).astype(BF16),
                             jnp.concatenate([kdl[j], bdl[j]], axis=0).astype(BF16)), 0.0)
           for j in subs]
    q_t = [(rd[j] - _dot(ay_cat[j], _stack_heads(tk[j], lane_head))).astype(BF16) for j in subs]
    y_0 = [_dot(cy_cat[j], v_st[j]) - _dot(ay_cat[j], _stack_heads(tcv[j], lane_head)) for j in subs]

    s_cur = [state[b_i] for b_i in range(nseq)]
    ys = [None] * len(subs)
    for jj in range(per_seq):
        for b_i in range(nseq):
            j = b_i * per_seq + jj
            s_b = s_cur[b_i].astype(BF16)
            ys[j] = _dot_nt(q_t[j], s_b) + y_0[j]
            s_cur[b_i] = s_cur[b_i] * e_tot[j] - _dot(s_b, g_m[j]) + h_m[j]
    for b_i in range(nseq):
        state[b_i] = s_cur[b_i]
    y = jnp.concatenate(ys, axis=0)
    yn = _group_norm(y, avg_ref[...], RWKV_GN_EPS, gnw_ref[...], gnb_ref[...])
    bonus = _head_sum(r * k * rk_ref[...], ones_ref[...]) * v
    out = ((yn + bonus) * gate).astype(o_ref.dtype)
    for b_i in range(nseq):
        o_ref[b_i] = out[b_i * rows:(b_i + 1) * rows]


def _ssd_kernel(h_ref, hn_ref, w_ref, wdt_ref, cw_ref, cb_ref, dtb_ref, alog_ref, dsk_ref, nw_ref,
                tri_ref, esel_ref, o_ref, cbuf, state, pbuf):
    rows = h_ref.shape[1]
    lane_head = _lane_head(rows)

    @pl.when(pl.program_id(1) == 0)
    def _():
        cbuf[:, 0:CARRY_ROWS, :] = jnp.zeros((2, CARRY_ROWS, cbuf.shape[2]), F32)
        state[...] = jnp.zeros_like(state)

    p, park = _pipelined_projection(h_ref, hn_ref, pbuf, [w_ref, wdt_ref])
    z = p[:, :GROUP]
    xbc = jax.nn.silu(_causal_conv(cbuf, p[:, GROUP:4 * GROUP], cw_ref[...], cb_ref[...], rows))
    xs = xbc[:, :GROUP]
    bm = xbc[:, GROUP:2 * GROUP].astype(BF16)
    cm = xbc[:, 2 * GROUP:].astype(BF16)
    dt = _softplus(p[:, 4 * GROUP:] + dtb_ref[...])
    log_a = dt * (-jnp.exp(alog_ref[...]))
    cum = _cumsum_rows(tri_ref[...], log_a)
    tot = cum[rows - 1:rows]
    vdt = xs * dt

    cum_rows = sum(_dot_nt(esel_ref[...], t) for t in _split(cum, 3))
    lane = lax.broadcasted_iota(jnp.int32, (rows, 128), 1)
    low = lane < HEAD
    cols = []
    for s in range(2):
        ch = cum[:, s * 128:(s + 1) * 128]
        sw = pltpu.roll(ch, HEAD, 1)
        cols.append(jnp.where(low, ch, sw))
        cols.append(jnp.where(low, sw, ch))
    t_i = lax.broadcasted_iota(jnp.int32, (rows, rows), 0)
    s_i = lax.broadcasted_iota(jnp.int32, (rows, rows), 1)
    causal = t_i >= s_i
    scores = [_dot_nt(cm[:, g * SSD_STATE:(g + 1) * SSD_STATE], bm[:, g * SSD_STATE:(g + 1) * SSD_STATE])
              for g in range(SSD_NGROUPS)]
    probs = []
    for hh in range(NHEAD):
        col = jnp.concatenate([cols[hh]] * (rows // 128), axis=1)
        row = jnp.broadcast_to(cum_rows[8 * hh:8 * hh + 1, :], (rows, rows))
        decay = jnp.exp(jnp.where(causal, col - row, NEG_BIG))
        probs.append((scores[hh // (NHEAD // SSD_NGROUPS)] * decay).astype(BF16))
    y = _dot(jnp.concatenate(probs, axis=1), _stack_heads(vdt, lane_head))
    y = y + _dot(cm, state[...].astype(BF16)) * jnp.exp(cum)
    upd = _dot_tn(bm, (vdt * jnp.exp(tot - cum)).astype(BF16))
    r_g = lax.broadcasted_iota(jnp.int32, (GROUP, GROUP), 0) // SSD_STATE
    c_g = lax.broadcasted_iota(jnp.int32, (GROUP, GROUP), 1) // SSD_STATE
    state[...] = state[...] * jnp.exp(tot) + jnp.where(r_g == c_g, upd, 0.0)
    y = (y + dsk_ref[...] * xs) * jax.nn.silu(z)
    outs = []
    for g in range(SSD_NGROUPS):
        yg = y[:, g * 128:(g + 1) * 128]
        outs.append(yg * lax.rsqrt(jnp.mean(yg * yg, axis=-1, keepdims=True) + NORM_EPS))
    o_ref[0] = (jnp.concatenate(outs, axis=1) * nw_ref[...]).astype(o_ref.dtype)
    park()


def _kv_kernel(m_ref, g_ref, wk_ref, wv_ref, k_ref, v_ref):
    h = _rms(m_ref[...], g_ref[...]).astype(BF16)
    k_ref[...] = _dot(h, wk_ref[...]).astype(k_ref.dtype)
    v_ref[...] = _dot(h, wv_ref[...]).astype(v_ref.dtype)


def _attn_kernel(ya_ref, yb_ref, yc_ref, yd_ref, x_ref, wmix_ref, g_ref, wq_ref, wo_ref,
                 k_ref, v_ref, o_ref):
    y = jnp.concatenate([ya_ref[0], yb_ref[0], yc_ref[0], yd_ref[0]], axis=1)
    x = x_ref[0] + _dot(y, wmix_ref[...])
    hd = x.shape[1] // MEM_HEADS
    q = _dot(_rms(x, g_ref[...]).astype(BF16), wq_ref[...])
    heads = [slice(hh * hd, (hh + 1) * hd) for hh in range(MEM_HEADS)]
    scores = [_dot_nt(q[:, sl].astype(BF16), k_ref[:, sl]) * (hd ** -0.5) for sl in heads]
    outs = []
    for s, sl in zip(scores, heads):
        e = jnp.exp(s - jnp.max(s, axis=-1, keepdims=True))
        pr = e / jnp.sum(e, axis=-1, keepdims=True)
        outs.append(_dot(pr.astype(BF16), v_ref[:, sl]))
    o = jnp.concatenate(outs, axis=1).astype(BF16)
    o_ref[0] = x + _dot(o, wo_ref[...])


def _ffn_kernel(x_ref, g_ref, wi_ref, wo_ref, gn_ref, *o_refs, hidden, last):
    x = x_ref[...]
    h = _rms(x, g_ref[...]).astype(BF16)
    acc = x
    for c in range(hidden // FFN_COLS):
        lo = c * FFN_COLS
        gate = _dot(h, wi_ref[:, lo:lo + FFN_COLS])
        up = _dot(h, wi_ref[:, hidden + lo:hidden + lo + FFN_COLS])
        acc = acc + _dot((jax.nn.silu(gate) * up).astype(BF16), wo_ref[lo:lo + FFN_COLS, :])
    normed = _rms(acc, gn_ref[...])
    if last:
        o_refs[0][...] = normed
    else:
        o_refs[0][...] = acc
        o_refs[1][...] = normed.astype(BF16)


def _layer_spec(layer, *tail):
    zeros = (0,) * len(tail)
    return pl.BlockSpec((None,) + tail, lambda *_: (layer,) + zeros)


def _const_spec(*shape):
    zeros = (0,) * len(shape)
    return pl.BlockSpec(shape, lambda *_: zeros)


def _mixer_call(body, name, layer, h, params, scratch, nseq=1, lookahead=False):
    bsz, seq, dm = h.shape
    chunk = min(SEQ_CHUNK, seq)
    in_specs = [pl.BlockSpec((nseq, chunk, dm), lambda b, c: (b, c, 0))]
    args = [h]
    if lookahead:
        last = seq // chunk - 1
        in_specs.append(
            pl.BlockSpec((nseq, chunk, dm), lambda b, c: (b, jnp.minimum(c + 1, last), 0)))
        args.append(h)
    for arr, kind in params:
        if kind == "layer":
            in_specs.append(_layer_spec(layer, *arr.shape[1:]))
        elif kind == "seq":
            in_specs.append(pl.BlockSpec((chunk, arr.shape[1]), lambda b, c: (c, 0)))
        else:
            in_specs.append(_const_spec(*arr.shape))
        args.append(arr)
    return pl.pallas_call(
        body,
        grid=(bsz // nseq, seq // chunk),
        in_specs=in_specs,
        out_specs=pl.BlockSpec((nseq, chunk, GROUP), lambda b, c: (b, c, 0)),
        out_shape=jax.ShapeDtypeStruct((bsz, seq, GROUP), BF16),
        scratch_shapes=scratch,
        compiler_params=pltpu.CompilerParams(
            dimension_semantics=("parallel", "arbitrary"), vmem_limit_bytes=VMEM_LIMIT),
        name=name,
    )(*args)


def _head_block_matrix(value):
    idx = jnp.arange(GROUP) // HEAD
    return jnp.where(idx[:, None] == idx[None, :], value, 0.0).astype(BF16)


def kernel(x, mem, norm_mix, w_in, lru_conv_w, lru_conv_b, lru_w_r, lru_b_r, lru_w_i, lru_b_i, lru_lambda, ret_gn_w, ret_gn_b, rwkv_mu, rwkv_w0, rwkv_w2, rwkv_a0, rwkv_a2, rwkv_g2, rwkv_k_k, rwkv_k_a, rwkv_r_k, rwkv_gn_w, rwkv_gn_b, ssd_conv_w, ssd_conv_b, ssd_dt_bias, ssd_a_log, ssd_d, ssd_norm_w, w_out, norm_mem_q, norm_mem_kv, mem_wq, mem_wk, mem_wv, mem_wo, norm_ffn, ffn_w_in, ffn_w_out, norm_final):
    bsz, seq, dm = x.shape
    depth = w_in.shape[0]
    n_mem = mem.shape[1]
    hidden = ffn_w_out.shape[1]
    chunk = min(SEQ_CHUNK, seq)
    tokens = bsz * seq
    row_tile = min(ROW_TILE, tokens)
    q_tile = min(ATTN_TILE, seq)
    rwkv_seqs = RWKV_SEQS if bsz % RWKV_SEQS == 0 else 1
    assert seq % chunk == 0 and chunk % RWKV_SUB == 0 and tokens % row_tile == 0
    assert seq % q_tile == 0 and hidden % FFN_COLS == 0

    def vec(a):
        return a.reshape(depth, 1, -1).astype(F32)

    def per_head(a):
        return jnp.repeat(a, HEAD, axis=1).reshape(depth, 1, GROUP).astype(F32)

    c_a, c_b, c_c = 2 * GROUP, 6 * GROUP, 10 * GROUP
    c_dt = c_c + 4 * GROUP
    w_a = w_in[:, :, :c_a].astype(BF16)
    w_b = w_in[:, :, c_a:c_b].astype(BF16)
    w_c = w_in[:, :, c_b:c_c].astype(BF16)
    w_d = w_in[:, :, c_c:c_dt].astype(BF16)
    w_dt = jnp.repeat(w_in[:, :, c_dt:], HEAD, axis=2).astype(BF16)
    eye_h = jnp.eye(NHEAD, dtype=F32)

    def block_diag(w):
        return jnp.einsum("lhij,hg->lhigj", w, eye_h).reshape(depth, GROUP, GROUP)

    lru_wg = jnp.concatenate([block_diag(lru_w_r), block_diag(lru_w_i)], axis=2).astype(BF16)
    lru_bg = jnp.concatenate([lru_b_r.reshape(depth, 1, GROUP), lru_b_i.reshape(depth, 1, GROUP)], axis=2)
    n_lo = rwkv_w2.shape[1]
    zeros_lo = jnp.zeros((depth, n_lo, GROUP), F32)
    rwkv_wa2 = jnp.concatenate([
        jnp.concatenate([rwkv_w2, zeros_lo], axis=2),
        jnp.concatenate([zeros_lo, zeros_lo], axis=2),
        jnp.concatenate([zeros_lo, zeros_lo], axis=2),
        jnp.concatenate([zeros_lo, rwkv_a2], axis=2),
    ], axis=1).astype(BF16)

    avg = _head_block_matrix(1.0 / HEAD)
    ones = _head_block_matrix(1.0)
    t_idx = jnp.arange(chunk)
    tri_full = (t_idx[:, None] >= t_idx[None, :]).astype(BF16)
    tri_sub = ((t_idx[:, None] >= t_idx[None, :])
               & (t_idx[:, None] // RWKV_SUB == t_idx[None, :] // RWKV_SUB)).astype(BF16)
    esel = (jnp.arange(GROUP)[None, :] == (jnp.arange(8 * NHEAD)[:, None] // 8) * HEAD).astype(BF16)

    pos = jnp.arange(seq, dtype=F32)
    inv_freq = ROPE_BASE ** (-jnp.arange(HEAD // 2, dtype=F32) / (HEAD // 2))
    ang = pos[:, None] * inv_freq[None, :]
    cos, sin = jnp.cos(ang), jnp.sin(ang)
    cos_t = jnp.tile(jnp.concatenate([cos, cos], axis=1), (1, NHEAD))
    sin_t = jnp.tile(jnp.concatenate([-sin, sin], axis=1), (1, NHEAD))

    w_out_b = w_out.astype(BF16)
    wq_b, wk_b, wv_b, wo_b = (a.astype(BF16) for a in (mem_wq, mem_wk, mem_wv, mem_wo))
    ffn_wi_b = ffn_w_in.astype(BF16)
    ffn_wo_b = ffn_w_out.astype(BF16)
    g_mix, g_q, g_kv, g_ffn = vec(norm_mix), vec(norm_mem_q), vec(norm_mem_kv), vec(norm_ffn)
    g_next = jnp.concatenate([g_mix[1:], norm_final.reshape(1, 1, dm).astype(F32)], axis=0)
    cparams = pltpu.CompilerParams(dimension_semantics=("parallel",), vmem_limit_bytes=VMEM_LIMIT)
    row = lambda width: pl.BlockSpec((row_tile, width), lambda i: (i, 0))

    mem_rows = bsz * n_mem
    kv_tile = min(ATTN_TILE, mem_rows)
    assert mem_rows % kv_tile == 0
    kv_shape = jax.ShapeDtypeStruct((depth, mem_rows, dm), BF16)
    kv_spec = pl.BlockSpec((None, kv_tile, dm), lambda l, i: (l, i, 0))
    w_spec = pl.BlockSpec((None, dm, dm), lambda l, i: (l, 0, 0))
    mem_k, mem_v = pl.pallas_call(
        _kv_kernel,
        grid=(depth, mem_rows // kv_tile),
        in_specs=[pl.BlockSpec((kv_tile, dm), lambda l, i: (i, 0)),
                  pl.BlockSpec((None, 1, dm), lambda l, i: (l, 0, 0)), w_spec, w_spec],
        out_specs=(kv_spec, kv_spec),
        out_shape=(kv_shape, kv_shape),
        compiler_params=pltpu.CompilerParams(
            dimension_semantics=("parallel", "parallel"), vmem_limit_bytes=VMEM_LIMIT),
        name="mem_kv",
    )(mem.reshape(mem_rows, dm), g_kv, wk_b, wv_b)
    mem_k = mem_k.reshape(depth, bsz, n_mem, dm)
    mem_v = mem_v.reshape(depth, bsz, n_mem, dm)

    h = pl.pallas_call(
        _prenorm_kernel,
        grid=(tokens // row_tile,),
        in_specs=[row(dm), _layer_spec(0, 1, dm)],
        out_specs=row(dm),
        out_shape=jax.ShapeDtypeStruct((tokens, dm), BF16),
        compiler_params=cparams,
        name="prenorm",
    )(x.reshape(tokens, dm), g_mix).reshape(bsz, seq, dm)

    conv_scr = lambda width: pltpu.VMEM((2, chunk + CARRY_ROWS, width), F32)
    sq_state = pltpu.VMEM((GROUP, GROUP), F32)

    for l in range(depth):
        lay = lambda a: (a, "layer")
        y_a = _mixer_call(
            _lru_kernel, "mix_lru", l, h,
            [lay(w_a), lay(lru_conv_w), lay(vec(lru_conv_b)), lay(lru_wg), lay(lru_bg),
             lay(vec(lru_lambda))],
            [conv_scr(GROUP), pltpu.VMEM((8, GROUP), F32)])
        y_b = _mixer_call(
            _ret_kernel, "mix_ret", l, h,
            [lay(w_b), (cos_t, "seq"), (sin_t, "seq"), lay(vec(ret_gn_w)),
             lay(vec(ret_gn_b)), (avg, "const")],
            [sq_state, pltpu.VMEM((NHEAD, chunk, chunk), F32), pltpu.VMEM((chunk, GROUP), F32),
             pltpu.VMEM((chunk, GROUP), F32), sq_state])
        y_c = _mixer_call(
            _rwkv_kernel, "mix_rwkv", l, h,
            [lay(w_c), lay(vec(rwkv_mu)), lay(vec(rwkv_w0)), lay(vec(rwkv_a0)),
             lay(rwkv_wa2), lay(rwkv_g2.astype(BF16)), lay(vec(rwkv_k_k)), lay(vec(rwkv_k_a)),
             lay(vec(rwkv_r_k)), lay(vec(rwkv_gn_w)), lay(vec(rwkv_gn_b)), (avg, "const"),
             (ones, "const"), (tri_sub, "const")],
            [pltpu.VMEM((rwkv_seqs, chunk + CARRY_ROWS, 4 * GROUP), F32),
             pltpu.VMEM((rwkv_seqs, GROUP, GROUP), F32)],
            nseq=rwkv_seqs)
        y_d = _mixer_call(
            _ssd_kernel, "mix_ssd", l, h,
            [lay(w_d), lay(w_dt), lay(ssd_conv_w), lay(vec(ssd_conv_b)),
             lay(per_head(ssd_dt_bias)), lay(per_head(ssd_a_log)), lay(per_head(ssd_d)),
             lay(vec(ssd_norm_w)), (tri_full, "const"), (esel, "const")],
            [conv_scr(3 * GROUP), sq_state, pltpu.VMEM((chunk, 5 * GROUP), F32)], lookahead=True)

        q_rows = lambda width: pl.BlockSpec((1, q_tile, width), lambda b, i: (b, i, 0))
        kv_l = pl.BlockSpec((None, None, n_mem, dm), lambda b, i, l=l: (l, b, 0, 0))
        x = pl.pallas_call(
            _attn_kernel,
            grid=(bsz, seq // q_tile),
            in_specs=[q_rows(GROUP)] * 4 + [
                q_rows(dm), _layer_spec(l, dm, dm), _layer_spec(l, 1, dm),
                _layer_spec(l, dm, dm), _layer_spec(l, dm, dm), kv_l, kv_l],
            out_specs=q_rows(dm),
            out_shape=jax.ShapeDtypeStruct((bsz, seq, dm), F32),
            compiler_params=pltpu.CompilerParams(
                dimension_semantics=("parallel", "parallel"), vmem_limit_bytes=VMEM_LIMIT),
            name="mem_attn",
        )(y_a, y_b, y_c, y_d, x, w_out_b, g_q, wq_b, wo_b, mem_k, mem_v)

        last = l == depth - 1
        x_shape = jax.ShapeDtypeStruct((tokens, dm), F32)
        outs = pl.pallas_call(
            functools.partial(_ffn_kernel, hidden=hidden, last=last),
            grid=(tokens // row_tile,),
            in_specs=[row(dm), _layer_spec(l, 1, dm),
                      pl.BlockSpec((None, dm, 2 * hidden), lambda i, l=l: (l, 0, 0),
                                   pipeline_mode=pl.Buffered(1)),
                      pl.BlockSpec((None, hidden, dm), lambda i, l=l: (l, 0, 0),
                                   pipeline_mode=pl.Buffered(1)),
                      _layer_spec(l, 1, dm)],
            out_specs=row(dm) if last else (row(dm), row(dm)),
            out_shape=x_shape if last else (x_shape, jax.ShapeDtypeStruct((tokens, dm), BF16)),
            compiler_params=cparams,
            name="ffn",
        )(x.reshape(tokens, dm), g_ffn, ffn_wi_b, ffn_wo_b, g_next)
        if last:
            x = outs.reshape(bsz, seq, dm)
        else:
            x = outs[0].reshape(bsz, seq, dm)
            h = outs[1].reshape(bsz, seq, dm)
    return x
```

```python
import functools
import math

import jax
import jax.numpy as jnp
from jax import lax
from jax.experimental import pallas as pl
from jax.experimental.pallas import tpu as pltpu

F32 = jnp.float32
BF16 = jnp.bfloat16

GROUP = 256
HEAD = 64
NHEAD = GROUP // HEAD
CONV_K = 4
LRU_C = 8.0
GN_EPS = 1e-5
RWKV_GN_EPS = HEAD * 1e-5
NORM_EPS = 1e-6
ROPE_BASE = 10000.0
SSD_NGROUPS = 2
SSD_STATE = 128
MEM_HEADS = 4
LOG_GAMMA = tuple(math.log1p(-(2.0 ** (-5.0 - h))) for h in range(NHEAD))

SEQ_CHUNK = 256
RWKV_SUB = 64
RWKV_SEQS = 4
LRU_CHUNK = 512
ROW_TILE = 1024
ATTN_TILE = 1024
FFN_COLS = 256
CARRY_ROWS = 8
VMEM_LIMIT = 56 * 1024 * 1024
NEG_BIG = -1e30


def _dot(a, b):
    return jnp.dot(a, b, preferred_element_type=F32)


def _dot_nt(a, b):
    return lax.dot_general(a, b, (((1,), (1,)), ((), ())), preferred_element_type=F32)


def _dot_tn(a, b):
    return lax.dot_general(a, b, (((0,), (0,)), ((), ())), preferred_element_type=F32)


def _split(x, n):
    terms = []
    for _ in range(n):
        t = x.astype(BF16)
        terms.append(t)
        x = x - t.astype(F32)
    return terms


def _cumsum_rows(tri, x):
    return sum(_dot(tri, t) for t in _split(x, 3))


def _rms(x, g):
    ms = jnp.mean(x * x, axis=-1, keepdims=True)
    return (x * lax.rsqrt(ms + NORM_EPS)) * g


def _softplus(x):
    return jnp.maximum(x, 0.0) + jnp.log1p(jnp.exp(-jnp.abs(x)))


def _gelu_tanh(x):
    return 0.5 * x * (1.0 + jnp.tanh(math.sqrt(2.0 / math.pi) * (x + 0.044715 * (x * x * x))))


def _lane_head(rows):
    return lax.broadcasted_iota(jnp.int32, (rows, GROUP), 1) // HEAD


def _stack_heads(z, lane_head):
    return jnp.concatenate(
        [jnp.where(lane_head == h, z, 0.0) for h in range(NHEAD)], axis=0).astype(BF16)


def _head_sum(x, block):
    return _dot(x.astype(BF16), block)


def _causal_conv(buf, cur, cw, cb, rows):
    assert cw.shape[0] == CONV_K == 4
    xb, bb = buf.at[0], buf.at[1]
    xb[CARRY_ROWS:CARRY_ROWS + rows, :] = cur
    prev = xb[CARRY_ROWS - 1:CARRY_ROWS - 1 + rows, :]
    bb[CARRY_ROWS:CARRY_ROWS + rows, :] = cw[1:2] * cur + cw[0:1] * prev
    y = (cw[3:4] * cur + cb) + cw[2:3] * prev + bb[CARRY_ROWS - 2:CARRY_ROWS - 2 + rows, :]
    xb[0:CARRY_ROWS, :] = xb[rows:rows + CARRY_ROWS, :]
    bb[0:CARRY_ROWS, :] = bb[rows:rows + CARRY_ROWS, :]
    return y


def _scan_rows(a, u, h0):
    rows, cols = a.shape
    sub = lax.broadcasted_iota(jnp.int32, (8, cols), 0)
    out = []
    h = h0
    for g in range(rows // 8):
        ag = a[g * 8:(g + 1) * 8]
        ug = u[g * 8:(g + 1) * 8]
        for d in (1, 2, 4):
            m = sub >= d
            ug = jnp.where(m, ag * pltpu.roll(ug, d, 0) + ug, ug)
            ag = jnp.where(m, ag * pltpu.roll(ag, d, 0), ag)
        hg = ug + ag * h
        out.append(hg)
        h = hg[7:8]
    return jnp.concatenate(out, axis=0), h


def _group_norm(y, avg, eps, w, b):
    mu = _head_sum(y, avg)
    d = y - mu
    var = _head_sum(d * d, avg)
    return (d * lax.rsqrt(var + eps)) * w + b


def _pipelined_projection(h_ref, hn_ref, pbuf, weights):
    @pl.when(pl.program_id(1) == 0)
    def _():
        off = 0
        for w in weights:
            pbuf[:, off:off + w.shape[1]] = _dot(h_ref[0], w[...])
            off += w.shape[1]

    p = pbuf[...]
    nxt = [_dot(hn_ref[0], w[...]) for w in weights]

    def park():
        off = 0
        for v in nxt:
            pbuf[:, off:off + v.shape[1]] = v
            off += v.shape[1]
    return p, park


def _prenorm_kernel(x_ref, g_ref, o_ref):
    o_ref[...] = _rms(x_ref[...], g_ref[...]).astype(o_ref.dtype)


def _lru_kernel(h_ref, w_ref, cw_ref, cb_ref, wg_ref, bg_ref, lam_ref, o_ref, cbuf, hcar):
    rows = h_ref.shape[1]

    @pl.when(pl.program_id(1) == 0)
    def _():
        cbuf[:, 0:CARRY_ROWS, :] = jnp.zeros((2, CARRY_ROWS, GROUP), F32)
        hcar[...] = jnp.zeros_like(hcar)

    p = _dot(h_ref[0], w_ref[...])
    gate = p[:, :GROUP]
    xr = _causal_conv(cbuf, p[:, GROUP:], cw_ref[...], cb_ref[...], rows)
    gates = _dot(xr.astype(BF16), wg_ref[...]) + bg_ref[...]
    r = jax.nn.sigmoid(gates[:, :GROUP])
    i = jax.nn.sigmoid(gates[:, GROUP:])
    log_a = (-LRU_C * r) * _softplus(-lam_ref[...])
    a = jnp.exp(log_a)
    z = -jnp.tanh(log_a) * (1.0 + a * a)
    u = jnp.where(z > 0.0, z * lax.rsqrt(z), 0.0) * (i * xr)
    hs, h_last = _scan_rows(a, u, hcar[0:1, :])
    hcar[0:1, :] = h_last
    o_ref[0] = (hs * _gelu_tanh(gate)).astype(o_ref.dtype)


def _rotary(z, cos, sin_signed):
    lane = lax.broadcasted_iota(jnp.int32, (z.shape[0], 128), 1)
    first = (lane % HEAD) < (HEAD // 2)
    halves = []
    for s in range(2):
        zh = z[:, s * 128:(s + 1) * 128]
        halves.append(jnp.where(first, pltpu.roll(zh, 128 - HEAD // 2, 1),
                                pltpu.roll(zh, HEAD // 2, 1)))
    return z * cos + jnp.concatenate(halves, axis=1) * sin_signed


def _ret_kernel(h_ref, w_ref, cos_ref, sin_ref, gnw_ref, gnb_ref, avg_ref, o_ref,
                state, dmask, qdec, kdec, sdec):
    rows = h_ref.shape[1]
    lane_head = _lane_head(rows)

    @pl.when(pl.program_id(1) == 0)
    def _():
        state[...] = jnp.zeros_like(state)
        t = lax.broadcasted_iota(jnp.int32, (rows, rows), 0)
        s = lax.broadcasted_iota(jnp.int32, (rows, rows), 1)
        dist = (t - s).astype(F32)
        for h in range(NHEAD):
            dmask[h] = jnp.where(t >= s, jnp.exp(dist * LOG_GAMMA[h]), 0.0)
        lg = jnp.zeros((rows, GROUP), F32)
        for h in range(NHEAD):
            lg = jnp.where(lane_head == h, LOG_GAMMA[h], lg)
        tt = lax.broadcasted_iota(jnp.int32, (rows, GROUP), 0).astype(F32)
        qdec[...] = jnp.exp((tt + 1.0) * lg)
        kdec[...] = jnp.exp((rows - 1.0 - tt) * lg)
        row_head = lax.broadcasted_iota(jnp.int32, (GROUP, GROUP), 0) // HEAD
        sd = jnp.zeros((GROUP, GROUP), F32)
        for h in range(NHEAD):
            sd = jnp.where(row_head == h, math.exp(rows * LOG_GAMMA[h]), sd)
        sdec[...] = sd

    p = _dot(h_ref[0], w_ref[...])
    q = _rotary(p[:, 0:GROUP], cos_ref[...], sin_ref[...])
    k = _rotary(p[:, GROUP:2 * GROUP], cos_ref[...], sin_ref[...]) * (HEAD ** -0.5)
    v = p[:, 2 * GROUP:3 * GROUP]
    gate = p[:, 3 * GROUP:]
    kb = k.astype(BF16)
    probs = []
    for hh in range(NHEAD):
        qm = jnp.where(lane_head == hh, q, 0.0).astype(BF16)
        probs.append((_dot_nt(qm, kb) * dmask[hh]).astype(BF16))
    y = _dot(jnp.concatenate(probs, axis=1), _stack_heads(v, lane_head))
    y = y + _dot((q * qdec[...]).astype(BF16), state[...].astype(BF16))
    kv = _dot_tn((k * kdec[...]).astype(BF16), v.astype(BF16))
    row_head = lax.broadcasted_iota(jnp.int32, (GROUP, GROUP), 0) // HEAD
    col_head = lax.broadcasted_iota(jnp.int32, (GROUP, GROUP), 1) // HEAD
    state[...] = state[...] * sdec[...] + jnp.where(row_head == col_head, kv, 0.0)
    yn = _group_norm(y, avg_ref[...], GN_EPS, gnw_ref[...], gnb_ref[...])
    o_ref[0] = (jax.nn.silu(gate) * yn).astype(o_ref.dtype)


def _rwkv_kernel(h_ref, w_ref, mu_ref, w0_ref, a0_ref, wa2_ref, g2_ref, kk_ref, ka_ref,
                 rk_ref, gnw_ref, gnb_ref, avg_ref, ones_ref, tri_ref, o_ref, sbuf, state):
    nseq, rows, dm = h_ref.shape
    sub = RWKV_SUB
    blk = NHEAD * sub
    per_seq = rows // sub
    lane_head = _lane_head(sub)

    @pl.when(pl.program_id(1) == 0)
    def _():
        for b_i in range(nseq):
            sbuf[b_i, 0:CARRY_ROWS, :] = jnp.zeros((CARRY_ROWS, sbuf.shape[2]), F32)
        state[...] = jnp.zeros_like(state)

    p = _dot(h_ref[...].reshape(nseq * rows, dm), w_ref[...])
    shifted = []
    for b_i in range(nseq):
        sbuf[b_i, CARRY_ROWS:CARRY_ROWS + rows, :] = p[b_i * rows:(b_i + 1) * rows]
        shifted.append(sbuf[b_i, CARRY_ROWS - 1:CARRY_ROWS - 1 + rows, :])
        sbuf[b_i, 0:CARRY_ROWS, :] = sbuf[b_i, rows:rows + CARRY_ROWS, :]
    p = p + (jnp.concatenate(shifted, axis=0) - p) * mu_ref[...]
    r = p[:, 0:GROUP]
    k = p[:, GROUP:2 * GROUP]
    v = p[:, 2 * GROUP:3 * GROUP]
    wa = p[:, 3 * GROUP:3 * GROUP + 128]
    g_lo = p[:, 3 * GROUP + 128:]
    lora = _dot(jnp.concatenate([jnp.tanh(wa), wa], axis=1).astype(BF16), wa2_ref[...])
    w_log = -_softplus(-(w0_ref[...] + lora[:, :GROUP])) - 0.5
    log_w = -jnp.exp(w_log)
    a = jax.nn.sigmoid(a0_ref[...] + lora[:, GROUP:])
    gate = _dot(jax.nn.sigmoid(g_lo).astype(BF16), g2_ref[...])
    kk = k * kk_ref[...]
    kk = kk * lax.rsqrt(_head_sum(kk * kk, ones_ref[...]) + 1e-12)
    k = k * (1.0 + (a - 1.0) * ka_ref[...])
    b = kk * a
    cl = jnp.concatenate(
        [_cumsum_rows(tri_ref[...], log_w[b_i * rows:(b_i + 1) * rows]) for b_i in range(nseq)],
        axis=0)

    t_f = lax.broadcasted_iota(jnp.int32, (sub, blk), 0)
    s_f = lax.broadcasted_iota(jnp.int32, (sub, blk), 1) % sub
    strict = t_f > s_f
    incl = t_f >= s_f
    eye = (t_f == s_f).astype(F32)
    r_i = lax.broadcasted_iota(jnp.int32, (blk, blk), 0)
    c_i = lax.broadcasted_iota(jnp.int32, (blk, blk), 1)
    same_head = (r_i // HEAD) == (c_i // HEAD)
    same_blk = (r_i // sub) == (c_i // sub)

    def unfold(f):
        return jnp.where(same_blk, jnp.concatenate([f] * NHEAD, axis=0), 0.0).astype(BF16)

    subs = range(nseq * per_seq)
    kkd, rd, bdl, kdl, e_tot, vs, v_st = [], [], [], [], [], [], []
    a_f, c_cat, ay_cat, cy_cat = [], [], [], []
    for j in subs:
        sl = slice(j * sub, (j + 1) * sub)
        clj = cl[sl]
        tot = clj[sub - 1:sub]
        e_neg = jnp.exp(-clj)
        e_end = jnp.exp(tot - clj)
        rd.append(r[sl] * jnp.exp(clj))
        kkd.append(kk[sl] * jnp.exp(clj - log_w[sl]))
        bd = b[sl] * e_neg
        kd = k[sl] * e_neg
        bdl.append(b[sl] * e_end)
        kdl.append(k[sl] * e_end)
        e_tot.append(jnp.exp(tot))
        vs.append(v[sl])
        v_st.append(_stack_heads(v[sl], lane_head))
        lhs = jnp.concatenate([kkd[j], rd[j]], axis=0).astype(BF16)
        rhs = jnp.concatenate([_stack_heads(bd, lane_head), _stack_heads(kd, lane_head)], axis=0)
        m = _dot_nt(lhs, rhs)
        a_f.append(jnp.where(strict, m[0:sub, 0:blk], 0.0))
        c_cat.append(jnp.where(strict, m[0:sub, blk:], 0.0).astype(BF16))
        ay_cat.append(jnp.where(incl, m[sub:, 0:blk], 0.0).astype(BF16))
        cy_cat.append(jnp.where(incl, m[sub:, blk:], 0.0).astype(BF16))
    t_f32 = [eye - a_ for a_ in a_f]
    pw = a_f
    pw_bd = [unfold(p_) for p_ in pw]
    for _ in range(int(math.log2(sub)) - 1):
        pw = [_dot(pw[j].astype(BF16), pw_bd[j]) for j in subs]
        pw_bd = [unfold(p_) for p_ in pw]
        t_f32 = [t_f32[j] + _dot(t_f32[j].astype(BF16), pw_bd[j]) for j in subs]
    t_cat = [t.astype(BF16) for t in t_f32]
    tk = [_dot(t_cat[j], _stack_heads(kkd[j], lane_head)) for j in subs]
    cv = [_dot(c_cat[j], v_st[j]) for j in subs]
    tcv = [_dot(t_cat[j], _stack_heads(cv[j], lane_head)) for j in subs]
    g_m = [jnp.where(same_head, _dot_tn(tk[j].astype(BF16), bdl[j].astype(BF16)), 0.0).astype(BF16)
           for j in subs]
    h_m = [jnp.where(same_head,
                     _dot_tn(jnp.concatenate([vs[j], -tcv[j]], axis=0).astype(BF16),
                             jnp.concatenate([kdl[j], bdl[j]], axis=0).astype(BF16)), 0.0)
           for j in subs]
    q_t = [(rd[j] - _dot(ay_cat[j], _stack_heads(tk[j], lane_head))).astype(BF16) for j in subs]
    y_0 = [_dot(cy_cat[j], v_st[j]) - _dot(ay_cat[j], _stack_heads(tcv[j], lane_head)) for j in subs]

    s_cur = [state[b_i] for b_i in range(nseq)]
    ys = [None] * len(subs)
    for jj in range(per_seq):
        for b_i in range(nseq):
            j = b_i * per_seq + jj
            s_b = s_cur[b_i].astype(BF16)
            ys[j] = _dot_nt(q_t[j], s_b) + y_0[j]
            s_cur[b_i] = s_cur[b_i] * e_tot[j] - _dot(s_b, g_m[j]) + h_m[j]
    for b_i in range(nseq):
        state[b_i] = s_cur[b_i]
    y = jnp.concatenate(ys, axis=0)
    yn = _group_norm(y, avg_ref[...], RWKV_GN_EPS, gnw_ref[...], gnb_ref[...])
    bonus = _head_sum(r * k * rk_ref[...], ones_ref[...]) * v
    out = ((yn + bonus) * gate).astype(o_ref.dtype)
    for b_i in range(nseq):
        o_ref[b_i] = out[b_i * rows:(b_i + 1) * rows]


def _ssd_kernel(h_ref, hn_ref, w_ref, wdt_ref, cw_ref, cb_ref, dtb_ref, alog_ref, dsk_ref, nw_ref,
                tri_ref, esel_ref, o_ref, cbuf, state, pbuf):
    rows = h_ref.shape[1]
    lane_head = _lane_head(rows)

    @pl.when(pl.program_id(1) == 0)
    def _():
        cbuf[:, 0:CARRY_ROWS, :] = jnp.zeros((2, CARRY_ROWS, cbuf.shape[2]), F32)
        state[...] = jnp.zeros_like(state)

    p, park = _pipelined_projection(h_ref, hn_ref, pbuf, [w_ref, wdt_ref])
    z = p[:, :GROUP]
    xbc = jax.nn.silu(_causal_conv(cbuf, p[:, GROUP:4 * GROUP], cw_ref[...], cb_ref[...], rows))
    xs = xbc[:, :GROUP]
    bm = xbc[:, GROUP:2 * GROUP].astype(BF16)
    cm = xbc[:, 2 * GROUP:].astype(BF16)
    dt = _softplus(p[:, 4 * GROUP:] + dtb_ref[...])
    log_a = dt * (-jnp.exp(alog_ref[...]))
    cum = _cumsum_rows(tri_ref[...], log_a)
    tot = cum[rows - 1:rows]
    vdt = xs * dt

    cum_rows = sum(_dot_nt(esel_ref[...], t) for t in _split(cum, 3))
    lane = lax.broadcasted_iota(jnp.int32, (rows, 128), 1)
    low = lane < HEAD
    cols = []
    for s in range(2):
        ch = cum[:, s * 128:(s + 1) * 128]
        sw = pltpu.roll(ch, HEAD, 1)
        cols.append(jnp.where(low, ch, sw))
        cols.append(jnp.where(low, sw, ch))
    t_i = lax.broadcasted_iota(jnp.int32, (rows, rows), 0)
    s_i = lax.broadcasted_iota(jnp.int32, (rows, rows), 1)
    causal = t_i >= s_i
    scores = [_dot_nt(cm[:, g * SSD_STATE:(g + 1) * SSD_STATE], bm[:, g * SSD_STATE:(g + 1) * SSD_STATE])
              for g in range(SSD_NGROUPS)]
    probs = []
    for hh in range(NHEAD):
        col = jnp.concatenate([cols[hh]] * (rows // 128), axis=1)
        row = jnp.broadcast_to(cum_rows[8 * hh:8 * hh + 1, :], (rows, rows))
        decay = jnp.exp(jnp.where(causal, col - row, NEG_BIG))
        probs.append((scores[hh // (NHEAD // SSD_NGROUPS)] * decay).astype(BF16))
    y = _dot(jnp.concatenate(probs, axis=1), _stack_heads(vdt, lane_head))
    y = y + _dot(cm, state[...].astype(BF16)) * jnp.exp(cum)
    upd = _dot_tn(bm, (vdt * jnp.exp(tot - cum)).astype(BF16))
    r_g = lax.broadcasted_iota(jnp.int32, (GROUP, GROUP), 0) // SSD_STATE
    c_g = lax.broadcasted_iota(jnp.int32, (GROUP, GROUP), 1) // SSD_STATE
    state[...] = state[...] * jnp.exp(tot) + jnp.where(r_g == c_g, upd, 0.0)
    y = (y + dsk_ref[...] * xs) * jax.nn.silu(z)
    outs = []
    for g in range(SSD_NGROUPS):
        yg = y[:, g * 128:(g + 1) * 128]
        outs.append(yg * lax.rsqrt(jnp.mean(yg * yg, axis=-1, keepdims=True) + NORM_EPS))
    o_ref[0] = (jnp.concatenate(outs, axis=1) * nw_ref[...]).astype(o_ref.dtype)
    park()


def _kv_kernel(m_ref, g_ref, wk_ref, wv_ref, k_ref, v_ref):
    h = _rms(m_ref[...], g_ref[...]).astype(BF16)
    k_ref[...] = _dot(h, wk_ref[...]).astype(k_ref.dtype)
    v_ref[...] = _dot(h, wv_ref[...]).astype(v_ref.dtype)


def _attn_kernel(ya_ref, yb_ref, yc_ref, yd_ref, x_ref, wmix_ref, g_ref, wq_ref, wo_ref,
                 k_ref, v_ref, o_ref):
    y = jnp.concatenate([ya_ref[0], yb_ref[0], yc_ref[0], yd_ref[0]], axis=1)
    x = x_ref[0] + _dot(y, wmix_ref[...])
    hd = x.shape[1] // MEM_HEADS
    q = _dot(_rms(x, g_ref[...]).astype(BF16), wq_ref[...])
    heads = [slice(hh * hd, (hh + 1) * hd) for hh in range(MEM_HEADS)]
    scores = [_dot_nt(q[:, sl].astype(BF16), k_ref[:, sl]) * (hd ** -0.5) for sl in heads]
    outs = []
    for s, sl in zip(scores, heads):
        e = jnp.exp(s - jnp.max(s, axis=-1, keepdims=True))
        pr = e / jnp.sum(e, axis=-1, keepdims=True)
        outs.append(_dot(pr.astype(BF16), v_ref[:, sl]))
    o = jnp.concatenate(outs, axis=1).astype(BF16)
    o_ref[0] = x + _dot(o, wo_ref[...])


def _ffn_kernel(x_ref, g_ref, wi_ref, wo_ref, gn_ref, *o_refs, hidden, last):
    x = x_ref[...]
    h = _rms(x, g_ref[...]).astype(BF16)
    acc = x
    for c in range(hidden // FFN_COLS):
        lo = c * FFN_COLS
        gate = _dot(h, wi_ref[:, lo:lo + FFN_COLS])
        up = _dot(h, wi_ref[:, hidden + lo:hidden + lo + FFN_COLS])
        acc = acc + _dot((jax.nn.silu(gate) * up).astype(BF16), wo_ref[lo:lo + FFN_COLS, :])
    normed = _rms(acc, gn_ref[...])
    if last:
        o_refs[0][...] = normed
    else:
        o_refs[0][...] = acc
        o_refs[1][...] = normed.astype(BF16)


def _layer_spec(layer, *tail):
    zeros = (0,) * len(tail)
    return pl.BlockSpec((None,) + tail, lambda *_: (layer,) + zeros)


def _const_spec(*shape):
    zeros = (0,) * len(shape)
    return pl.BlockSpec(shape, lambda *_: zeros)


def _mixer_call(body, name, layer, h, params, scratch, nseq=1, lookahead=False, chunk=SEQ_CHUNK):
    bsz, seq, dm = h.shape
    chunk = min(chunk, seq)
    assert seq % chunk == 0
    in_specs = [pl.BlockSpec((nseq, chunk, dm), lambda b, c: (b, c, 0))]
    args = [h]
    if lookahead:
        last = seq // chunk - 1
        in_specs.append(
            pl.BlockSpec((nseq, chunk, dm), lambda b, c: (b, jnp.minimum(c + 1, last), 0)))
        args.append(h)
    for arr, kind in params:
        if kind == "layer":
            in_specs.append(_layer_spec(layer, *arr.shape[1:]))
        elif kind == "seq":
            in_specs.append(pl.BlockSpec((chunk, arr.shape[1]), lambda b, c: (c, 0)))
        else:
            in_specs.append(_const_spec(*arr.shape))
        args.append(arr)
    return pl.pallas_call(
        body,
        grid=(bsz // nseq, seq // chunk),
        in_specs=in_specs,
        out_specs=pl.BlockSpec((nseq, chunk, GROUP), lambda b, c: (b, c, 0)),
        out_shape=jax.ShapeDtypeStruct((bsz, seq, GROUP), BF16),
        scratch_shapes=scratch,
        compiler_params=pltpu.CompilerParams(
            dimension_semantics=("parallel", "arbitrary"), vmem_limit_bytes=VMEM_LIMIT),
        name=name,
    )(*args)


def _head_block_matrix(value):
    idx = jnp.arange(GROUP) // HEAD
    return jnp.where(idx[:, None] == idx[None, :], value, 0.0).astype(BF16)


def kernel(x, mem, norm_mix, w_in, lru_conv_w, lru_conv_b, lru_w_r, lru_b_r, lru_w_i, lru_b_i, lru_lambda, ret_gn_w, ret_gn_b, rwkv_mu, rwkv_w0, rwkv_w2, rwkv_a0, rwkv_a2, rwkv_g2, rwkv_k_k, rwkv_k_a, rwkv_r_k, rwkv_gn_w, rwkv_gn_b, ssd_conv_w, ssd_conv_b, ssd_dt_bias, ssd_a_log, ssd_d, ssd_norm_w, w_out, norm_mem_q, norm_mem_kv, mem_wq, mem_wk, mem_wv, mem_wo, norm_ffn, ffn_w_in, ffn_w_out, norm_final):
    bsz, seq, dm = x.shape
    depth = w_in.shape[0]
    n_mem = mem.shape[1]
    hidden = ffn_w_out.shape[1]
    chunk = min(SEQ_CHUNK, seq)
    tokens = bsz * seq
    row_tile = min(ROW_TILE, tokens)
    q_tile = min(ATTN_TILE, seq)
    rwkv_seqs = RWKV_SEQS if bsz % RWKV_SEQS == 0 else 1
    assert seq % chunk == 0 and chunk % RWKV_SUB == 0 and tokens % row_tile == 0
    assert seq % q_tile == 0 and hidden % FFN_COLS == 0

    def vec(a):
        return a.reshape(depth, 1, -1).astype(F32)

    def per_head(a):
        return jnp.repeat(a, HEAD, axis=1).reshape(depth, 1, GROUP).astype(F32)

    c_a, c_b, c_c = 2 * GROUP, 6 * GROUP, 10 * GROUP
    c_dt = c_c + 4 * GROUP
    w_a = w_in[:, :, :c_a].astype(BF16)
    w_b = w_in[:, :, c_a:c_b].astype(BF16)
    w_c = w_in[:, :, c_b:c_c].astype(BF16)
    w_d = w_in[:, :, c_c:c_dt].astype(BF16)
    w_dt = jnp.repeat(w_in[:, :, c_dt:], HEAD, axis=2).astype(BF16)
    eye_h = jnp.eye(NHEAD, dtype=F32)

    def block_diag(w):
        return jnp.einsum("lhij,hg->lhigj", w, eye_h).reshape(depth, GROUP, GROUP)

    lru_wg = jnp.concatenate([block_diag(lru_w_r), block_diag(lru_w_i)], axis=2).astype(BF16)
    lru_bg = jnp.concatenate([lru_b_r.reshape(depth, 1, GROUP), lru_b_i.reshape(depth, 1, GROUP)], axis=2)
    n_lo = rwkv_w2.shape[1]
    zeros_lo = jnp.zeros((depth, n_lo, GROUP), F32)
    rwkv_wa2 = jnp.concatenate([
        jnp.concatenate([rwkv_w2, zeros_lo], axis=2),
        jnp.concatenate([zeros_lo, zeros_lo], axis=2),
        jnp.concatenate([zeros_lo, zeros_lo], axis=2),
        jnp.concatenate([zeros_lo, rwkv_a2], axis=2),
    ], axis=1).astype(BF16)

    avg = _head_block_matrix(1.0 / HEAD)
    ones = _head_block_matrix(1.0)
    t_idx = jnp.arange(chunk)
    tri_full = (t_idx[:, None] >= t_idx[None, :]).astype(BF16)
    tri_sub = ((t_idx[:, None] >= t_idx[None, :])
               & (t_idx[:, None] // RWKV_SUB == t_idx[None, :] // RWKV_SUB)).astype(BF16)
    esel = (jnp.arange(GROUP)[None, :] == (jnp.arange(8 * NHEAD)[:, None] // 8) * HEAD).astype(BF16)

    pos = jnp.arange(seq, dtype=F32)
    inv_freq = ROPE_BASE ** (-jnp.arange(HEAD // 2, dtype=F32) / (HEAD // 2))
    ang = pos[:, None] * inv_freq[None, :]
    cos, sin = jnp.cos(ang), jnp.sin(ang)
    cos_t = jnp.tile(jnp.concatenate([cos, cos], axis=1), (1, NHEAD))
    sin_t = jnp.tile(jnp.concatenate([-sin, sin], axis=1), (1, NHEAD))

    w_out_b = w_out.astype(BF16)
    wq_b, wk_b, wv_b, wo_b = (a.astype(BF16) for a in (mem_wq, mem_wk, mem_wv, mem_wo))
    ffn_wi_b = ffn_w_in.astype(BF16)
    ffn_wo_b = ffn_w_out.astype(BF16)
    g_mix, g_q, g_kv, g_ffn = vec(norm_mix), vec(norm_mem_q), vec(norm_mem_kv), vec(norm_ffn)
    g_next = jnp.concatenate([g_mix[1:], norm_final.reshape(1, 1, dm).astype(F32)], axis=0)
    cparams = pltpu.CompilerParams(dimension_semantics=("parallel",), vmem_limit_bytes=VMEM_LIMIT)
    row = lambda width: pl.BlockSpec((row_tile, width), lambda i: (i, 0))

    mem_rows = bsz * n_mem
    kv_tile = min(ATTN_TILE, mem_rows)
    assert mem_rows % kv_tile == 0
    kv_shape = jax.ShapeDtypeStruct((depth, mem_rows, dm), BF16)
    kv_spec = pl.BlockSpec((None, kv_tile, dm), lambda l, i: (l, i, 0))
    w_spec = pl.BlockSpec((None, dm, dm), lambda l, i: (l, 0, 0))
    mem_k, mem_v = pl.pallas_call(
        _kv_kernel,
        grid=(depth, mem_rows // kv_tile),
        in_specs=[pl.BlockSpec((kv_tile, dm), lambda l, i: (i, 0)),
                  pl.BlockSpec((None, 1, dm), lambda l, i: (l, 0, 0)), w_spec, w_spec],
        out_specs=(kv_spec, kv_spec),
        out_shape=(kv_shape, kv_shape),
        compiler_params=pltpu.CompilerParams(
            dimension_semantics=("parallel", "parallel"), vmem_limit_bytes=VMEM_LIMIT),
        name="mem_kv",
    )(mem.reshape(mem_rows, dm), g_kv, wk_b, wv_b)
    mem_k = mem_k.reshape(depth, bsz, n_mem, dm)
    mem_v = mem_v.reshape(depth, bsz, n_mem, dm)

    h = pl.pallas_call(
        _prenorm_kernel,
        grid=(tokens // row_tile,),
        in_specs=[row(dm), _layer_spec(0, 1, dm)],
        out_specs=row(dm),
        out_shape=jax.ShapeDtypeStruct((tokens, dm), BF16),
        compiler_params=cparams,
        name="prenorm",
    )(x.reshape(tokens, dm), g_mix).reshape(bsz, seq, dm)

    lru_chunk = min(LRU_CHUNK, seq)
    conv_scr = lambda width: pltpu.VMEM((2, chunk + CARRY_ROWS, width), F32)
    sq_state = pltpu.VMEM((GROUP, GROUP), F32)

    for l in range(depth):
        lay = lambda a: (a, "layer")
        y_a = _mixer_call(
            _lru_kernel, "mix_lru", l, h,
            [lay(w_a), lay(lru_conv_w), lay(vec(lru_conv_b)), lay(lru_wg), lay(lru_bg),
             lay(vec(lru_lambda))],
            [pltpu.VMEM((2, lru_chunk + CARRY_ROWS, GROUP), F32), pltpu.VMEM((8, GROUP), F32)],
            chunk=lru_chunk)
        y_b = _mixer_call(
            _ret_kernel, "mix_ret", l, h,
            [lay(w_b), (cos_t, "seq"), (sin_t, "seq"), lay(vec(ret_gn_w)),
             lay(vec(ret_gn_b)), (avg, "const")],
            [sq_state, pltpu.VMEM((NHEAD, chunk, chunk), F32), pltpu.VMEM((chunk, GROUP), F32),
             pltpu.VMEM((chunk, GROUP), F32), sq_state])
        y_c = _mixer_call(
            _rwkv_kernel, "mix_rwkv", l, h,
            [lay(w_c), lay(vec(rwkv_mu)), lay(vec(rwkv_w0)), lay(vec(rwkv_a0)),
             lay(rwkv_wa2), lay(rwkv_g2.astype(BF16)), lay(vec(rwkv_k_k)), lay(vec(rwkv_k_a)),
             lay(vec(rwkv_r_k)), lay(vec(rwkv_gn_w)), lay(vec(rwkv_gn_b)), (avg, "const"),
             (ones, "const"), (tri_sub, "const")],
            [pltpu.VMEM((rwkv_seqs, chunk + CARRY_ROWS, 4 * GROUP), F32),
             pltpu.VMEM((rwkv_seqs, GROUP, GROUP), F32)],
            nseq=rwkv_seqs)
        y_d = _mixer_call(
            _ssd_kernel, "mix_ssd", l, h,
            [lay(w_d), lay(w_dt), lay(ssd_conv_w), lay(vec(ssd_conv_b)),
             lay(per_head(ssd_dt_bias)), lay(per_head(ssd_a_log)), lay(per_head(ssd_d)),
             lay(vec(ssd_norm_w)), (tri_full, "const"), (esel, "const")],
            [conv_scr(3 * GROUP), sq_state, pltpu.VMEM((chunk, 5 * GROUP), F32)], lookahead=True)

        q_rows = lambda width: pl.BlockSpec((1, q_tile, width), lambda b, i: (b, i, 0))
        kv_l = pl.BlockSpec((None, None, n_mem, dm), lambda b, i, l=l: (l, b, 0, 0))
        x = pl.pallas_call(
            _attn_kernel,
            grid=(bsz, seq // q_tile),
            in_specs=[q_rows(GROUP)] * 4 + [
                q_rows(dm), _layer_spec(l, dm, dm), _layer_spec(l, 1, dm),
                _layer_spec(l, dm, dm), _layer_spec(l, dm, dm), kv_l, kv_l],
            out_specs=q_rows(dm),
            out_shape=jax.ShapeDtypeStruct((bsz, seq, dm), F32),
            compiler_params=pltpu.CompilerParams(
                dimension_semantics=("parallel", "parallel"), vmem_limit_bytes=VMEM_LIMIT),
            name="mem_attn",
        )(y_a, y_b, y_c, y_d, x, w_out_b, g_q, wq_b, wo_b, mem_k, mem_v)

        last = l == depth - 1
        x_shape = jax.ShapeDtypeStruct((tokens, dm), F32)
        outs = pl.pallas_call(
            functools.partial(_ffn_kernel, hidden=hidden, last=last),
            grid=(tokens // row_tile,),
            in_specs=[row(dm), _layer_spec(l, 1, dm),
                      pl.BlockSpec((None, dm, 2 * hidden), lambda i, l=l: (l, 0, 0),
                                   pipeline_mode=pl.Buffered(1)),
                      pl.BlockSpec((None, hidden, dm), lambda i, l=l: (l, 0, 0),
                                   pipeline_mode=pl.Buffered(1)),
                      _layer_spec(l, 1, dm)],
            out_specs=row(dm) if last else (row(dm), row(dm)),
            out_shape=x_shape if last else (x_shape, jax.ShapeDtypeStruct((tokens, dm), BF16)),
            compiler_params=cparams,
            name="ffn",
        )(x.reshape(tokens, dm), g_ffn, ffn_wi_b, ffn_wo_b, g_next)
        if last:
            x = outs.reshape(bsz, seq, dm)
        else:
            x = outs[0].reshape(bsz, seq, dm)
            h = outs[1].reshape(bsz, seq, dm)
    return x
```

```python
import functools
import math

import jax
import jax.numpy as jnp
from jax import lax
from jax.experimental import pallas as pl
from jax.experimental.pallas import tpu as pltpu

F32 = jnp.float32
BF16 = jnp.bfloat16

GROUP = 256
HEAD = 64
NHEAD = GROUP // HEAD
CONV_K = 4
LRU_C = 8.0
GN_EPS = 1e-5
RWKV_GN_EPS = HEAD * 1e-5
NORM_EPS = 1e-6
ROPE_BASE = 10000.0
SSD_NGROUPS = 2
SSD_STATE = 128
MEM_HEADS = 4
LOG_GAMMA = tuple(math.log1p(-(2.0 ** (-5.0 - h))) for h in range(NHEAD))

SEQ_CHUNK = 256
RWKV_SUB = 64
RWKV_SEQS = 4
LRU_CHUNK = 2048
ROW_TILE = 1024
ATTN_TILE = 1024
FFN_COLS = 256
CARRY_ROWS = 8
VMEM_LIMIT = 56 * 1024 * 1024
NEG_BIG = -1e30


def _dot(a, b):
    return jnp.dot(a, b, preferred_element_type=F32)


def _dot_nt(a, b):
    return lax.dot_general(a, b, (((1,), (1,)), ((), ())), preferred_element_type=F32)


def _dot_tn(a, b):
    return lax.dot_general(a, b, (((0,), (0,)), ((), ())), preferred_element_type=F32)


def _split(x, n):
    terms = []
    for _ in range(n):
        t = x.astype(BF16)
        terms.append(t)
        x = x - t.astype(F32)
    return terms


def _cumsum_rows(tri, x):
    return sum(_dot(tri, t) for t in _split(x, 3))


def _rms(x, g):
    ms = jnp.mean(x * x, axis=-1, keepdims=True)
    return (x * lax.rsqrt(ms + NORM_EPS)) * g


def _softplus(x):
    return jnp.maximum(x, 0.0) + jnp.log1p(jnp.exp(-jnp.abs(x)))


def _gelu_tanh(x):
    return 0.5 * x * (1.0 + jnp.tanh(math.sqrt(2.0 / math.pi) * (x + 0.044715 * (x * x * x))))


def _lane_head(rows):
    return lax.broadcasted_iota(jnp.int32, (rows, GROUP), 1) // HEAD


def _stack_heads(z, lane_head):
    return jnp.concatenate(
        [jnp.where(lane_head == h, z, 0.0) for h in range(NHEAD)], axis=0).astype(BF16)


def _head_sum(x, block):
    return _dot(x.astype(BF16), block)


def _causal_conv(buf, cur, cw, cb, rows):
    assert cw.shape[0] == CONV_K == 4
    xb, bb = buf.at[0], buf.at[1]
    xb[CARRY_ROWS:CARRY_ROWS + rows, :] = cur
    prev = xb[CARRY_ROWS - 1:CARRY_ROWS - 1 + rows, :]
    bb[CARRY_ROWS:CARRY_ROWS + rows, :] = cw[1:2] * cur + cw[0:1] * prev
    y = (cw[3:4] * cur + cb) + cw[2:3] * prev + bb[CARRY_ROWS - 2:CARRY_ROWS - 2 + rows, :]
    xb[0:CARRY_ROWS, :] = xb[rows:rows + CARRY_ROWS, :]
    bb[0:CARRY_ROWS, :] = bb[rows:rows + CARRY_ROWS, :]
    return y


def _scan_rows(a, u, h0):
    rows, cols = a.shape
    sub = lax.broadcasted_iota(jnp.int32, (8, cols), 0)
    out = []
    h = h0
    for g in range(rows // 8):
        ag = a[g * 8:(g + 1) * 8]
        ug = u[g * 8:(g + 1) * 8]
        for d in (1, 2, 4):
            m = sub >= d
            ug = jnp.where(m, ag * pltpu.roll(ug, d, 0) + ug, ug)
            ag = jnp.where(m, ag * pltpu.roll(ag, d, 0), ag)
        hg = ug + ag * h
        out.append(hg)
        h = hg[7:8]
    return jnp.concatenate(out, axis=0), h


def _group_norm(y, avg, eps, w, b):
    mu = _head_sum(y, avg)
    d = y - mu
    var = _head_sum(d * d, avg)
    return (d * lax.rsqrt(var + eps)) * w + b


def _pipelined_projection(h_ref, hn_ref, pbuf, weights):
    @pl.when(pl.program_id(1) == 0)
    def _():
        off = 0
        for w in weights:
            pbuf[:, off:off + w.shape[1]] = _dot(h_ref[0], w[...])
            off += w.shape[1]

    p = pbuf[...]
    nxt = [_dot(hn_ref[0], w[...]) for w in weights]

    def park():
        off = 0
        for v in nxt:
            pbuf[:, off:off + v.shape[1]] = v
            off += v.shape[1]
    return p, park


def _prenorm_kernel(x_ref, g_ref, o_ref):
    o_ref[...] = _rms(x_ref[...], g_ref[...]).astype(o_ref.dtype)


def _lru_kernel(h_ref, w_ref, cw_ref, cb_ref, wg_ref, bg_ref, lam_ref, o_ref, cbuf, hcar):
    rows = h_ref.shape[1]

    @pl.when(pl.program_id(1) == 0)
    def _():
        cbuf[:, 0:CARRY_ROWS, :] = jnp.zeros((2, CARRY_ROWS, GROUP), F32)
        hcar[...] = jnp.zeros_like(hcar)

    p = _dot(h_ref[0], w_ref[...])
    gate = p[:, :GROUP]
    xr = _causal_conv(cbuf, p[:, GROUP:], cw_ref[...], cb_ref[...], rows)
    gates = _dot(xr.astype(BF16), wg_ref[...]) + bg_ref[...]
    r = jax.nn.sigmoid(gates[:, :GROUP])
    i = jax.nn.sigmoid(gates[:, GROUP:])
    log_a = (-LRU_C * r) * _softplus(-lam_ref[...])
    a = jnp.exp(log_a)
    z = -jnp.tanh(log_a) * (1.0 + a * a)
    u = jnp.where(z > 0.0, z * lax.rsqrt(z), 0.0) * (i * xr)
    hs, h_last = _scan_rows(a, u, hcar[0:1, :])
    hcar[0:1, :] = h_last
    o_ref[0] = (hs * _gelu_tanh(gate)).astype(o_ref.dtype)


def _rotary(z, cos, sin_signed):
    lane = lax.broadcasted_iota(jnp.int32, (z.shape[0], 128), 1)
    first = (lane % HEAD) < (HEAD // 2)
    halves = []
    for s in range(2):
        zh = z[:, s * 128:(s + 1) * 128]
        halves.append(jnp.where(first, pltpu.roll(zh, 128 - HEAD // 2, 1),
                                pltpu.roll(zh, HEAD // 2, 1)))
    return z * cos + jnp.concatenate(halves, axis=1) * sin_signed


def _ret_kernel(h_ref, w_ref, cos_ref, sin_ref, gnw_ref, gnb_ref, avg_ref, o_ref,
                state, dmask, qdec, kdec, sdec):
    rows = h_ref.shape[1]
    lane_head = _lane_head(rows)

    @pl.when(pl.program_id(1) == 0)
    def _():
        state[...] = jnp.zeros_like(state)
        t = lax.broadcasted_iota(jnp.int32, (rows, rows), 0)
        s = lax.broadcasted_iota(jnp.int32, (rows, rows), 1)
        dist = (t - s).astype(F32)
        for h in range(NHEAD):
            dmask[h] = jnp.where(t >= s, jnp.exp(dist * LOG_GAMMA[h]), 0.0)
        lg = jnp.zeros((rows, GROUP), F32)
        for h in range(NHEAD):
            lg = jnp.where(lane_head == h, LOG_GAMMA[h], lg)
        tt = lax.broadcasted_iota(jnp.int32, (rows, GROUP), 0).astype(F32)
        qdec[...] = jnp.exp((tt + 1.0) * lg)
        kdec[...] = jnp.exp((rows - 1.0 - tt) * lg)
        row_head = lax.broadcasted_iota(jnp.int32, (GROUP, GROUP), 0) // HEAD
        sd = jnp.zeros((GROUP, GROUP), F32)
        for h in range(NHEAD):
            sd = jnp.where(row_head == h, math.exp(rows * LOG_GAMMA[h]), sd)
        sdec[...] = sd

    p = _dot(h_ref[0], w_ref[...])
    q = _rotary(p[:, 0:GROUP], cos_ref[...], sin_ref[...])
    k = _rotary(p[:, GROUP:2 * GROUP], cos_ref[...], sin_ref[...]) * (HEAD ** -0.5)
    v = p[:, 2 * GROUP:3 * GROUP]
    gate = p[:, 3 * GROUP:]
    kb = k.astype(BF16)
    probs = []
    for hh in range(NHEAD):
        qm = jnp.where(lane_head == hh, q, 0.0).astype(BF16)
        probs.append((_dot_nt(qm, kb) * dmask[hh]).astype(BF16))
    y = _dot(jnp.concatenate(probs, axis=1), _stack_heads(v, lane_head))
    y = y + _dot((q * qdec[...]).astype(BF16), state[...].astype(BF16))
    kv = _dot_tn((k * kdec[...]).astype(BF16), v.astype(BF16))
    row_head = lax.broadcasted_iota(jnp.int32, (GROUP, GROUP), 0) // HEAD
    col_head = lax.broadcasted_iota(jnp.int32, (GROUP, GROUP), 1) // HEAD
    state[...] = state[...] * sdec[...] + jnp.where(row_head == col_head, kv, 0.0)
    yn = _group_norm(y, avg_ref[...], GN_EPS, gnw_ref[...], gnb_ref[...])
    o_ref[0] = (jax.nn.silu(gate) * yn).astype(o_ref.dtype)


def _rwkv_kernel(h_ref, w_ref, mu_ref, w0_ref, a0_ref, wa2_ref, g2_ref, kk_ref, ka_ref,
                 rk_ref, gnw_ref, gnb_ref, avg_ref, ones_ref, tri_ref, o_ref, sbuf, state):
    nseq, rows, dm = h_ref.shape
    sub = RWKV_SUB
    blk = NHEAD * sub
    per_seq = rows // sub
    lane_head = _lane_head(sub)

    @pl.when(pl.program_id(1) == 0)
    def _():
        for b_i in range(nseq):
            sbuf[b_i, 0:CARRY_ROWS, :] = jnp.zeros((CARRY_ROWS, sbuf.shape[2]), F32)
        state[...] = jnp.zeros_like(state)

    p = _dot(h_ref[...].reshape(nseq * rows, dm), w_ref[...])
    shifted = []
    for b_i in range(nseq):
        sbuf[b_i, CARRY_ROWS:CARRY_ROWS + rows, :] = p[b_i * rows:(b_i + 1) * rows]
        shifted.append(sbuf[b_i, CARRY_ROWS - 1:CARRY_ROWS - 1 + rows, :])
        sbuf[b_i, 0:CARRY_ROWS, :] = sbuf[b_i, rows:rows + CARRY_ROWS, :]
    p = p + (jnp.concatenate(shifted, axis=0) - p) * mu_ref[...]
    r = p[:, 0:GROUP]
    k = p[:, GROUP:2 * GROUP]
    v = p[:, 2 * GROUP:3 * GROUP]
    wa = p[:, 3 * GROUP:3 * GROUP + 128]
    g_lo = p[:, 3 * GROUP + 128:]
    lora = _dot(jnp.concatenate([jnp.tanh(wa), wa], axis=1).astype(BF16), wa2_ref[...])
    w_log = -_softplus(-(w0_ref[...] + lora[:, :GROUP])) - 0.5
    log_w = -jnp.exp(w_log)
    a = jax.nn.sigmoid(a0_ref[...] + lora[:, GROUP:])
    gate = _dot(jax.nn.sigmoid(g_lo).astype(BF16), g2_ref[...])
    kk = k * kk_ref[...]
    kk = kk * lax.rsqrt(_head_sum(kk * kk, ones_ref[...]) + 1e-12)
    k = k * (1.0 + (a - 1.0) * ka_ref[...])
    b = kk * a
    cl = jnp.concatenate(
        [_cumsum_rows(tri_ref[...], log_w[b_i * rows:(b_i + 1) * rows]) for b_i in range(nseq)],
        axis=0)

    t_f = lax.broadcasted_iota(jnp.int32, (sub, blk), 0)
    s_f = lax.broadcasted_iota(jnp.int32, (sub, blk), 1) % sub
    strict = t_f > s_f
    incl = t_f >= s_f
    eye = (t_f == s_f).astype(F32)
    r_i = lax.broadcasted_iota(jnp.int32, (blk, blk), 0)
    c_i = lax.broadcasted_iota(jnp.int32, (blk, blk), 1)
    same_head = (r_i // HEAD) == (c_i // HEAD)
    same_blk = (r_i // sub) == (c_i // sub)

    def unfold(f):
        return jnp.where(same_blk, jnp.concatenate([f] * NHEAD, axis=0), 0.0).astype(BF16)

    subs = range(nseq * per_seq)
    kkd, rd, bdl, kdl, e_tot, vs, v_st = [], [], [], [], [], [], []
    a_f, c_cat, ay_cat, cy_cat = [], [], [], []
    for j in subs:
        sl = slice(j * sub, (j + 1) * sub)
        clj = cl[sl]
        tot = clj[sub - 1:sub]
        e_neg = jnp.exp(-clj)
        e_end = jnp.exp(tot - clj)
        rd.append(r[sl] * jnp.exp(clj))
        kkd.append(kk[sl] * jnp.exp(clj - log_w[sl]))
        bd = b[sl] * e_neg
        kd = k[sl] * e_neg
        bdl.append(b[sl] * e_end)
        kdl.append(k[sl] * e_end)
        e_tot.append(jnp.exp(tot))
        vs.append(v[sl])
        v_st.append(_stack_heads(v[sl], lane_head))
        lhs = jnp.concatenate([kkd[j], rd[j]], axis=0).astype(BF16)
        rhs = jnp.concatenate([_stack_heads(bd, lane_head), _stack_heads(kd, lane_head)], axis=0)
        m = _dot_nt(lhs, rhs)
        a_f.append(jnp.where(strict, m[0:sub, 0:blk], 0.0))
        c_cat.append(jnp.where(strict, m[0:sub, blk:], 0.0).astype(BF16))
        ay_cat.append(jnp.where(incl, m[sub:, 0:blk], 0.0).astype(BF16))
        cy_cat.append(jnp.where(incl, m[sub:, blk:], 0.0).astype(BF16))
    t_f32 = [eye - a_ for a_ in a_f]
    pw = a_f
    pw_bd = [unfold(p_) for p_ in pw]
    for _ in range(int(math.log2(sub)) - 1):
        pw = [_dot(pw[j].astype(BF16), pw_bd[j]) for j in subs]
        pw_bd = [unfold(p_) for p_ in pw]
        t_f32 = [t_f32[j] + _dot(t_f32[j].astype(BF16), pw_bd[j]) for j in subs]
    t_cat = [t.astype(BF16) for t in t_f32]
    tk = [_dot(t_cat[j], _stack_heads(kkd[j], lane_head)) for j in subs]
    cv = [_dot(c_cat[j], v_st[j]) for j in subs]
    tcv = [_dot(t_cat[j], _stack_heads(cv[j], lane_head)) for j in subs]
    g_m = [jnp.where(same_head, _dot_tn(tk[j].astype(BF16), bdl[j].astype(BF16)), 0.0).astype(BF16)
           for j in subs]
    h_m = [jnp.where(same_head,
                     _dot_tn(jnp.concatenate([vs[j], -tcv[j]], axis=0).astype(BF16),
                             jnp.concatenate([kdl[j], bdl[j]], axis=0).astype(BF16)), 0.0)
           for j in subs]
    q_t = [(rd[j] - _dot(ay_cat[j], _stack_heads(tk[j], lane_head))).astype(BF16) for j in subs]
    y_0 = [_dot(cy_cat[j], v_st[j]) - _dot(ay_cat[j], _stack_heads(tcv[j], lane_head)) for j in subs]

    s_cur = [state[b_i] for b_i in range(nseq)]
    ys = [None] * len(subs)
    for jj in range(per_seq):
        for b_i in range(nseq):
            j = b_i * per_seq + jj
            s_b = s_cur[b_i].astype(BF16)
            ys[j] = _dot_nt(q_t[j], s_b) + y_0[j]
            s_cur[b_i] = s_cur[b_i] * e_tot[j] - _dot(s_b, g_m[j]) + h_m[j]
    for b_i in range(nseq):
        state[b_i] = s_cur[b_i]
    y = jnp.concatenate(ys, axis=0)
    yn = _group_norm(y, avg_ref[...], RWKV_GN_EPS, gnw_ref[...], gnb_ref[...])
    bonus = _head_sum(r * k * rk_ref[...], ones_ref[...]) * v
    out = ((yn + bonus) * gate).astype(o_ref.dtype)
    for b_i in range(nseq):
        o_ref[b_i] = out[b_i * rows:(b_i + 1) * rows]


def _ssd_kernel(h_ref, hn_ref, w_ref, wdt_ref, cw_ref, cb_ref, dtb_ref, alog_ref, dsk_ref, nw_ref,
                tri_ref, esel_ref, o_ref, cbuf, state, pbuf):
    rows = h_ref.shape[1]
    lane_head = _lane_head(rows)

    @pl.when(pl.program_id(1) == 0)
    def _():
        cbuf[:, 0:CARRY_ROWS, :] = jnp.zeros((2, CARRY_ROWS, cbuf.shape[2]), F32)
        state[...] = jnp.zeros_like(state)

    p, park = _pipelined_projection(h_ref, hn_ref, pbuf, [w_ref, wdt_ref])
    z = p[:, :GROUP]
    xbc = jax.nn.silu(_causal_conv(cbuf, p[:, GROUP:4 * GROUP], cw_ref[...], cb_ref[...], rows))
    xs = xbc[:, :GROUP]
    bm = xbc[:, GROUP:2 * GROUP].astype(BF16)
    cm = xbc[:, 2 * GROUP:].astype(BF16)
    dt = _softplus(p[:, 4 * GROUP:] + dtb_ref[...])
    log_a = dt * (-jnp.exp(alog_ref[...]))
    cum = _cumsum_rows(tri_ref[...], log_a)
    tot = cum[rows - 1:rows]
    vdt = xs * dt

    cum_rows = sum(_dot_nt(esel_ref[...], t) for t in _split(cum, 3))
    lane = lax.broadcasted_iota(jnp.int32, (rows, 128), 1)
    low = lane < HEAD
    cols = []
    for s in range(2):
        ch = cum[:, s * 128:(s + 1) * 128]
        sw = pltpu.roll(ch, HEAD, 1)
        cols.append(jnp.where(low, ch, sw))
        cols.append(jnp.where(low, sw, ch))
    t_i = lax.broadcasted_iota(jnp.int32, (rows, rows), 0)
    s_i = lax.broadcasted_iota(jnp.int32, (rows, rows), 1)
    causal = t_i >= s_i
    scores = [_dot_nt(cm[:, g * SSD_STATE:(g + 1) * SSD_STATE], bm[:, g * SSD_STATE:(g + 1) * SSD_STATE])
              for g in range(SSD_NGROUPS)]
    probs = []
    for hh in range(NHEAD):
        col = jnp.concatenate([cols[hh]] * (rows // 128), axis=1)
        row = jnp.broadcast_to(cum_rows[8 * hh:8 * hh + 1, :], (rows, rows))
        decay = jnp.exp(jnp.where(causal, col - row, NEG_BIG))
        probs.append((scores[hh // (NHEAD // SSD_NGROUPS)] * decay).astype(BF16))
    y = _dot(jnp.concatenate(probs, axis=1), _stack_heads(vdt, lane_head))
    y = y + _dot(cm, state[...].astype(BF16)) * jnp.exp(cum)
    upd = _dot_tn(bm, (vdt * jnp.exp(tot - cum)).astype(BF16))
    r_g = lax.broadcasted_iota(jnp.int32, (GROUP, GROUP), 0) // SSD_STATE
    c_g = lax.broadcasted_iota(jnp.int32, (GROUP, GROUP), 1) // SSD_STATE
    state[...] = state[...] * jnp.exp(tot) + jnp.where(r_g == c_g, upd, 0.0)
    y = (y + dsk_ref[...] * xs) * jax.nn.silu(z)
    outs = []
    for g in range(SSD_NGROUPS):
        yg = y[:, g * 128:(g + 1) * 128]
        outs.append(yg * lax.rsqrt(jnp.mean(yg * yg, axis=-1, keepdims=True) + NORM_EPS))
    o_ref[0] = (jnp.concatenate(outs, axis=1) * nw_ref[...]).astype(o_ref.dtype)
    park()


def _kv_kernel(m_ref, g_ref, wk_ref, wv_ref, k_ref, v_ref):
    h = _rms(m_ref[...], g_ref[...]).astype(BF16)
    k_ref[...] = _dot(h, wk_ref[...]).astype(k_ref.dtype)
    v_ref[...] = _dot(h, wv_ref[...]).astype(v_ref.dtype)


def _attn_kernel(ya_ref, yb_ref, yc_ref, yd_ref, x_ref, wmix_ref, g_ref, wq_ref, wo_ref,
                 k_ref, v_ref, o_ref):
    y = jnp.concatenate([ya_ref[0], yb_ref[0], yc_ref[0], yd_ref[0]], axis=1)
    x = x_ref[0] + _dot(y, wmix_ref[...])
    hd = x.shape[1] // MEM_HEADS
    q = _dot(_rms(x, g_ref[...]).astype(BF16), wq_ref[...])
    heads = [slice(hh * hd, (hh + 1) * hd) for hh in range(MEM_HEADS)]
    scores = [_dot_nt(q[:, sl].astype(BF16), k_ref[:, sl]) * (hd ** -0.5) for sl in heads]
    outs = []
    for s, sl in zip(scores, heads):
        e = jnp.exp(s - jnp.max(s, axis=-1, keepdims=True))
        pr = e / jnp.sum(e, axis=-1, keepdims=True)
        outs.append(_dot(pr.astype(BF16), v_ref[:, sl]))
    o = jnp.concatenate(outs, axis=1).astype(BF16)
    o_ref[0] = x + _dot(o, wo_ref[...])


def _ffn_kernel(x_ref, g_ref, wi_ref, wo_ref, gn_ref, *o_refs, hidden, last):
    x = x_ref[...]
    h = _rms(x, g_ref[...]).astype(BF16)
    acc = x
    for c in range(hidden // FFN_COLS):
        lo = c * FFN_COLS
        gate = _dot(h, wi_ref[:, lo:lo + FFN_COLS])
        up = _dot(h, wi_ref[:, hidden + lo:hidden + lo + FFN_COLS])
        acc = acc + _dot((jax.nn.silu(gate) * up).astype(BF16), wo_ref[lo:lo + FFN_COLS, :])
    normed = _rms(acc, gn_ref[...])
    if last:
        o_refs[0][...] = normed
    else:
        o_refs[0][...] = acc
        o_refs[1][...] = normed.astype(BF16)


def _layer_spec(layer, *tail):
    zeros = (0,) * len(tail)
    return pl.BlockSpec((None,) + tail, lambda *_: (layer,) + zeros)


def _const_spec(*shape):
    zeros = (0,) * len(shape)
    return pl.BlockSpec(shape, lambda *_: zeros)


def _mixer_call(body, name, layer, h, params, scratch, nseq=1, lookahead=False, chunk=SEQ_CHUNK):
    bsz, seq, dm = h.shape
    chunk = min(chunk, seq)
    assert seq % chunk == 0
    in_specs = [pl.BlockSpec((nseq, chunk, dm), lambda b, c: (b, c, 0))]
    args = [h]
    if lookahead:
        last = seq // chunk - 1
        in_specs.append(
            pl.BlockSpec((nseq, chunk, dm), lambda b, c: (b, jnp.minimum(c + 1, last), 0)))
        args.append(h)
    for arr, kind in params:
        if kind == "layer":
            in_specs.append(_layer_spec(layer, *arr.shape[1:]))
        elif kind == "seq":
            in_specs.append(pl.BlockSpec((chunk, arr.shape[1]), lambda b, c: (c, 0)))
        else:
            in_specs.append(_const_spec(*arr.shape))
        args.append(arr)
    return pl.pallas_call(
        body,
        grid=(bsz // nseq, seq // chunk),
        in_specs=in_specs,
        out_specs=pl.BlockSpec((nseq, chunk, GROUP), lambda b, c: (b, c, 0)),
        out_shape=jax.ShapeDtypeStruct((bsz, seq, GROUP), BF16),
        scratch_shapes=scratch,
        compiler_params=pltpu.CompilerParams(
            dimension_semantics=("parallel", "arbitrary"), vmem_limit_bytes=VMEM_LIMIT),
        name=name,
    )(*args)


def _head_block_matrix(value):
    idx = jnp.arange(GROUP) // HEAD
    return jnp.where(idx[:, None] == idx[None, :], value, 0.0).astype(BF16)


def kernel(x, mem, norm_mix, w_in, lru_conv_w, lru_conv_b, lru_w_r, lru_b_r, lru_w_i, lru_b_i, lru_lambda, ret_gn_w, ret_gn_b, rwkv_mu, rwkv_w0, rwkv_w2, rwkv_a0, rwkv_a2, rwkv_g2, rwkv_k_k, rwkv_k_a, rwkv_r_k, rwkv_gn_w, rwkv_gn_b, ssd_conv_w, ssd_conv_b, ssd_dt_bias, ssd_a_log, ssd_d, ssd_norm_w, w_out, norm_mem_q, norm_mem_kv, mem_wq, mem_wk, mem_wv, mem_wo, norm_ffn, ffn_w_in, ffn_w_out, norm_final):
    bsz, seq, dm = x.shape
    depth = w_in.shape[0]
    n_mem = mem.shape[1]
    hidden = ffn_w_out.shape[1]
    chunk = min(SEQ_CHUNK, seq)
    tokens = bsz * seq
    row_tile = min(ROW_TILE, tokens)
    q_tile = min(ATTN_TILE, seq)
    rwkv_seqs = RWKV_SEQS if bsz % RWKV_SEQS == 0 else 1
    assert seq % chunk == 0 and chunk % RWKV_SUB == 0 and tokens % row_tile == 0
    assert seq % q_tile == 0 and hidden % FFN_COLS == 0

    def vec(a):
        return a.reshape(depth, 1, -1).astype(F32)

    def per_head(a):
        return jnp.repeat(a, HEAD, axis=1).reshape(depth, 1, GROUP).astype(F32)

    c_a, c_b, c_c = 2 * GROUP, 6 * GROUP, 10 * GROUP
    c_dt = c_c + 4 * GROUP
    w_a = w_in[:, :, :c_a].astype(BF16)
    w_b = w_in[:, :, c_a:c_b].astype(BF16)
    w_c = w_in[:, :, c_b:c_c].astype(BF16)
    w_d = w_in[:, :, c_c:c_dt].astype(BF16)
    w_dt = jnp.repeat(w_in[:, :, c_dt:], HEAD, axis=2).astype(BF16)
    eye_h = jnp.eye(NHEAD, dtype=F32)

    def block_diag(w):
        return jnp.einsum("lhij,hg->lhigj", w, eye_h).reshape(depth, GROUP, GROUP)

    lru_wg = jnp.concatenate([block_diag(lru_w_r), block_diag(lru_w_i)], axis=2).astype(BF16)
    lru_bg = jnp.concatenate([lru_b_r.reshape(depth, 1, GROUP), lru_b_i.reshape(depth, 1, GROUP)], axis=2)
    n_lo = rwkv_w2.shape[1]
    zeros_lo = jnp.zeros((depth, n_lo, GROUP), F32)
    rwkv_wa2 = jnp.concatenate([
        jnp.concatenate([rwkv_w2, zeros_lo], axis=2),
        jnp.concatenate([zeros_lo, zeros_lo], axis=2),
        jnp.concatenate([zeros_lo, zeros_lo], axis=2),
        jnp.concatenate([zeros_lo, rwkv_a2], axis=2),
    ], axis=1).astype(BF16)

    avg = _head_block_matrix(1.0 / HEAD)
    ones = _head_block_matrix(1.0)
    t_idx = jnp.arange(chunk)
    tri_full = (t_idx[:, None] >= t_idx[None, :]).astype(BF16)
    tri_sub = ((t_idx[:, None] >= t_idx[None, :])
               & (t_idx[:, None] // RWKV_SUB == t_idx[None, :] // RWKV_SUB)).astype(BF16)
    esel = (jnp.arange(GROUP)[None, :] == (jnp.arange(8 * NHEAD)[:, None] // 8) * HEAD).astype(BF16)

    pos = jnp.arange(seq, dtype=F32)
    inv_freq = ROPE_BASE ** (-jnp.arange(HEAD // 2, dtype=F32) / (HEAD // 2))
    ang = pos[:, None] * inv_freq[None, :]
    cos, sin = jnp.cos(ang), jnp.sin(ang)
    cos_t = jnp.tile(jnp.concatenate([cos, cos], axis=1), (1, NHEAD))
    sin_t = jnp.tile(jnp.concatenate([-sin, sin], axis=1), (1, NHEAD))

    w_out_b = w_out.astype(BF16)
    wq_b, wk_b, wv_b, wo_b = (a.astype(BF16) for a in (mem_wq, mem_wk, mem_wv, mem_wo))
    ffn_wi_b = ffn_w_in.astype(BF16)
    ffn_wo_b = ffn_w_out.astype(BF16)
    g_mix, g_q, g_kv, g_ffn = vec(norm_mix), vec(norm_mem_q), vec(norm_mem_kv), vec(norm_ffn)
    g_next = jnp.concatenate([g_mix[1:], norm_final.reshape(1, 1, dm).astype(F32)], axis=0)
    cparams = pltpu.CompilerParams(dimension_semantics=("parallel",), vmem_limit_bytes=VMEM_LIMIT)
    row = lambda width: pl.BlockSpec((row_tile, width), lambda i: (i, 0))

    mem_rows = bsz * n_mem
    kv_tile = min(ATTN_TILE, mem_rows)
    assert mem_rows % kv_tile == 0
    kv_shape = jax.ShapeDtypeStruct((depth, mem_rows, dm), BF16)
    kv_spec = pl.BlockSpec((None, kv_tile, dm), lambda l, i: (l, i, 0))
    w_spec = pl.BlockSpec((None, dm, dm), lambda l, i: (l, 0, 0))
    mem_k, mem_v = pl.pallas_call(
        _kv_kernel,
        grid=(depth, mem_rows // kv_tile),
        in_specs=[pl.BlockSpec((kv_tile, dm), lambda l, i: (i, 0)),
                  pl.BlockSpec((None, 1, dm), lambda l, i: (l, 0, 0)), w_spec, w_spec],
        out_specs=(kv_spec, kv_spec),
        out_shape=(kv_shape, kv_shape),
        compiler_params=pltpu.CompilerParams(
            dimension_semantics=("parallel", "parallel"), vmem_limit_bytes=VMEM_LIMIT),
        name="mem_kv",
    )(mem.reshape(mem_rows, dm), g_kv, wk_b, wv_b)
    mem_k = mem_k.reshape(depth, bsz, n_mem, dm)
    mem_v = mem_v.reshape(depth, bsz, n_mem, dm)

    h = pl.pallas_call(
        _prenorm_kernel,
        grid=(tokens // row_tile,),
        in_specs=[row(dm), _layer_spec(0, 1, dm)],
        out_specs=row(dm),
        out_shape=jax.ShapeDtypeStruct((tokens, dm), BF16),
        compiler_params=cparams,
        name="prenorm",
    )(x.reshape(tokens, dm), g_mix).reshape(bsz, seq, dm)

    lru_chunk = min(LRU_CHUNK, seq)
    conv_scr = lambda width: pltpu.VMEM((2, chunk + CARRY_ROWS, width), F32)
    sq_state = pltpu.VMEM((GROUP, GROUP), F32)

    for l in range(depth):
        lay = lambda a: (a, "layer")
        y_a = _mixer_call(
            _lru_kernel, "mix_lru", l, h,
            [lay(w_a), lay(lru_conv_w), lay(vec(lru_conv_b)), lay(lru_wg), lay(lru_bg),
             lay(vec(lru_lambda))],
            [pltpu.VMEM((2, lru_chunk + CARRY_ROWS, GROUP), F32), pltpu.VMEM((8, GROUP), F32)],
            chunk=lru_chunk)
        y_b = _mixer_call(
            _ret_kernel, "mix_ret", l, h,
            [lay(w_b), (cos_t, "seq"), (sin_t, "seq"), lay(vec(ret_gn_w)),
             lay(vec(ret_gn_b)), (avg, "const")],
            [sq_state, pltpu.VMEM((NHEAD, chunk, chunk), F32), pltpu.VMEM((chunk, GROUP), F32),
             pltpu.VMEM((chunk, GROUP), F32), sq_state])
        y_c = _mixer_call(
            _rwkv_kernel, "mix_rwkv", l, h,
            [lay(w_c), lay(vec(rwkv_mu)), lay(vec(rwkv_w0)), lay(vec(rwkv_a0)),
             lay(rwkv_wa2), lay(rwkv_g2.astype(BF16)), lay(vec(rwkv_k_k)), lay(vec(rwkv_k_a)),
             lay(vec(rwkv_r_k)), lay(vec(rwkv_gn_w)), lay(vec(rwkv_gn_b)), (avg, "const"),
             (ones, "const"), (tri_sub, "const")],
            [pltpu.VMEM((rwkv_seqs, chunk + CARRY_ROWS, 4 * GROUP), F32),
             pltpu.VMEM((rwkv_seqs, GROUP, GROUP), F32)],
            nseq=rwkv_seqs)
        y_d = _mixer_call(
            _ssd_kernel, "mix_ssd", l, h,
            [lay(w_d), lay(w_dt), lay(ssd_conv_w), lay(vec(ssd_conv_b)),
             lay(per_head(ssd_dt_bias)), lay(per_head(ssd_a_log)), lay(per_head(ssd_d)),
             lay(vec(ssd_norm_w)), (tri_full, "const"), (esel, "const")],
            [conv_scr(3 * GROUP), sq_state, pltpu.VMEM((chunk, 5 * GROUP), F32)], lookahead=True)

        q_rows = lambda width: pl.BlockSpec((1, q_tile, width), lambda b, i: (b, i, 0))
        kv_l = pl.BlockSpec((None, None, n_mem, dm), lambda b, i, l=l: (l, b, 0, 0))
        x = pl.pallas_call(
            _attn_kernel,
            grid=(bsz, seq // q_tile),
            in_specs=[q_rows(GROUP)] * 4 + [
                q_rows(dm), _layer_spec(l, dm, dm), _layer_spec(l, 1, dm),
                _layer_spec(l, dm, dm), _layer_spec(l, dm, dm), kv_l, kv_l],
            out_specs=q_rows(dm),
            out_shape=jax.ShapeDtypeStruct((bsz, seq, dm), F32),
            compiler_params=pltpu.CompilerParams(
                dimension_semantics=("parallel", "parallel"), vmem_limit_bytes=VMEM_LIMIT),
            name="mem_attn",
        )(y_a, y_b, y_c, y_d, x, w_out_b, g_q, wq_b, wo_b, mem_k, mem_v)

        last = l == depth - 1
        x_shape = jax.ShapeDtypeStruct((tokens, dm), F32)
        outs = pl.pallas_call(
            functools.partial(_ffn_kernel, hidden=hidden, last=last),
            grid=(tokens // row_tile,),
            in_specs=[row(dm), _layer_spec(l, 1, dm),
                      pl.BlockSpec((None, dm, 2 * hidden), lambda i, l=l: (l, 0, 0),
                                   pipeline_mode=pl.Buffered(1)),
                      pl.BlockSpec((None, hidden, dm), lambda i, l=l: (l, 0, 0),
                                   pipeline_mode=pl.Buffered(1)),
                      _layer_spec(l, 1, dm)],
            out_specs=row(dm) if last else (row(dm), row(dm)),
            out_shape=x_shape if last else (x_shape, jax.ShapeDtypeStruct((tokens, dm), BF16)),
            compiler_params=cparams,
            name="ffn",
        )(x.reshape(tokens, dm), g_ffn, ffn_wi_b, ffn_wo_b, g_next)
        if last:
            x = outs.reshape(bsz, seq, dm)
        else:
            x = outs[0].reshape(bsz, seq, dm)
            h = outs[1].reshape(bsz, seq, dm)
    return x
```
